```python
import jax, jax.numpy as jnp
from jax import lax
import numpy as np

D_MODEL = 1024
BATCH = 4
SEQ = 4096
DEPTH = 2

CHUNK = 64
N_META = 16
Q_BLOCK = 128
D_MIX = D_MODEL
FOX_HEADS = 8
FOX_HEAD_DIM = 64
FOX_WIDTH = FOX_HEADS * FOX_HEAD_DIM
HG_HEADS = 4
HG_EXPAND = 128
HG_HEAD_V = (D_MIX - FOX_WIDTH) // HG_HEADS
HG_K = HG_HEADS * HG_EXPAND
HG_V = HG_HEADS * HG_HEAD_V
D_FF = -(-8 * D_MODEL // (3 * 256)) * 256
EPS = 1e-6
MASK_VALUE = -1e30
LOG_F_MIN = -30.0
IN_SIZES = (FOX_WIDTH, FOX_WIDTH, FOX_WIDTH, FOX_HEADS, HG_K, HG_K, HG_V, HG_V)
IN_COLS = sum(IN_SIZES)

kernel_name = "hymba_fox_hgrn2_hybrid_trunk"


def rms_norm(x, w):
    xf = x.astype(jnp.float32)
    y = xf * lax.rsqrt(jnp.mean(xf * xf, axis=-1, keepdims=True) + EPS)
    return (y * w.astype(jnp.float32)).astype(x.dtype)


def forgetting_attention(q, k, v, log_f):
    B, L, H, Dh = q.shape
    n_blk = -(-L // Q_BLOCK)
    Lp = n_blk * Q_BLOCK
    pad = Lp - L
    padf = lambda a: jnp.pad(a, [(0, 0), (0, pad)] + [(0, 0)] * (a.ndim - 2))
    q, k, v, log_f = padf(q), padf(k), padf(v), padf(log_f)
    c = jnp.cumsum(log_f, axis=1)
    cT = c.transpose(0, 2, 1)
    scale = Dh ** -0.5
    key_pos = jnp.arange(Lp)
    qb = q.reshape(B, n_blk, Q_BLOCK, H, Dh).transpose(1, 0, 2, 3, 4)
    cb = c.reshape(B, n_blk, Q_BLOCK, H).transpose(1, 0, 3, 2)

    def block(args):
        i, q_i, c_i = args
        s = jnp.einsum('bqhd,bkhd->bhqk', q_i, k, preferred_element_type=jnp.float32) * scale
        s = s + (c_i[..., :, None] - cT[:, :, None, :])
        q_pos = i * Q_BLOCK + jnp.arange(Q_BLOCK)
        s = jnp.where(key_pos[None, :] <= q_pos[:, None], s, MASK_VALUE)
        p = jax.nn.softmax(s, axis=-1)
        return jnp.einsum('bhqk,bkhd->bqhd', p.astype(v.dtype), v)

    o = lax.map(block, (jnp.arange(n_blk), qb, cb))
    return o.transpose(1, 0, 2, 3, 4).reshape(B, Lp, H, Dh)[:, :L]


def hgrn2_recurrence(q, k, v, log_f):
    B, L, H, K = q.shape
    V = v.shape[-1]
    pad = (-L) % CHUNK
    padf = lambda a: jnp.pad(a, ((0, 0), (pad, 0), (0, 0), (0, 0)))
    q, k, v, log_f = padf(q), padf(k), padf(v), padf(log_f)
    Lp = L + pad
    n = Lp // CHUNK
    to_chunks = lambda a: a.reshape(B, n, CHUNK, H, a.shape[-1]).transpose(1, 0, 3, 2, 4)
    causal = jnp.tril(jnp.ones((CHUNK, CHUNK), dtype=bool))[:, :, None]

    def step(S, inp):
        q_c, k_c, v_c, g_c = inp
        b = jnp.cumsum(g_c, axis=2)
        o_inter = jnp.einsum('bhtk,bhkv->bhtv', q_c * jnp.exp(b), S)
        rel = b[:, :, :, None, :] - b[:, :, None, :, :]
        decay = jnp.where(causal, jnp.exp(jnp.where(causal, rel, 0.0)), 0.0)
        A = jnp.einsum('bhtk,bhsk,bhtsk->bhts', q_c, k_c, decay)
        o_intra = jnp.einsum('bhts,bhsv->bhtv', A, v_c)
        b_last = b[:, :, -1]
        k_dec = k_c * jnp.exp(b_last[:, :, None, :] - b)
        S = jnp.exp(b_last)[..., None] * S + jnp.einsum('bhsk,bhsv->bhkv', k_dec, v_c)
        return S, o_inter + o_intra

    S0 = jnp.zeros((B, H, K, V), jnp.float32)
    _, o = lax.scan(step, S0, (to_chunks(q), to_chunks(k), to_chunks(v), to_chunks(log_f)))
    o = o.transpose(1, 0, 3, 2, 4).reshape(B, Lp, H, V)
    return o[:, pad:]


def setup_inputs(seed: int = 0) -> dict:
    key = jax.random.key(seed)
    ks = jax.random.split(key, 14)
    nrm = lambda k, shape, s: jax.random.normal(k, shape, jnp.float32) * s
    return {
        "x": nrm(ks[0], (BATCH, SEQ, D_MODEL), 1.0),
        "meta": nrm(ks[1], (N_META, D_MODEL), 1.0),
        "norm_mix_w": 1.0 + nrm(ks[2], (DEPTH, D_MODEL), 0.02),
        "w_in": nrm(ks[3], (DEPTH, D_MODEL, IN_COLS), D_MODEL ** -0.5),
        "fox_f_bias": 2.0 + nrm(ks[4], (DEPTH, FOX_HEADS), 0.5),
        "hgrn_lb_raw": nrm(ks[5], (DEPTH, HG_K), 1.0),
        "hgrn_norm_w": 1.0 + nrm(ks[6], (DEPTH, HG_HEAD_V), 0.02),
        "w_out": nrm(ks[7], (DEPTH, D_MIX, D_MODEL), D_MIX ** -0.5),
        "norm_ffn_w": 1.0 + nrm(ks[8], (DEPTH, D_MODEL), 0.02),
        "w_ffn_gate": nrm(ks[9], (DEPTH, D_MODEL, D_FF), D_MODEL ** -0.5),
        "w_ffn_up": nrm(ks[10], (DEPTH, D_MODEL, D_FF), D_MODEL ** -0.5),
        "w_ffn_down": nrm(ks[11], (DEPTH, D_FF, D_MODEL), D_FF ** -0.5),
        "norm_final_w": 1.0 + nrm(ks[12], (D_MODEL,), 0.02),
    }


def reference(x, meta, norm_mix_w, w_in, fox_f_bias, hgrn_lb_raw, hgrn_norm_w, w_out,
              norm_ffn_w, w_ffn_gate, w_ffn_up, w_ffn_down, norm_final_w):
    B = x.shape[0]
    h = jnp.concatenate([jnp.broadcast_to(meta[None].astype(x.dtype), (B, N_META, D_MODEL)), x], axis=1)
    L = h.shape[1]
    s_lb = jax.nn.softmax(hgrn_lb_raw.astype(jnp.float32), axis=0)
    lower_bounds = jnp.cumsum(s_lb, axis=0) - s_lb[0]
    split_at = tuple(int(v) for v in np.cumsum(IN_SIZES)[:-1])

    for l in range(DEPTH):
        u = rms_norm(h, norm_mix_w[l])
        proj = u @ w_in[l]
        fq, fk, fv, ff, hq, hf, hi, hg = jnp.split(proj, split_at, axis=-1)

        fox_log_f = jax.nn.log_sigmoid(ff.astype(jnp.float32) + fox_f_bias[l].astype(jnp.float32))
        hs = (B, L, FOX_HEADS, FOX_HEAD_DIM)
        fox_out = forgetting_attention(fq.reshape(hs), fk.reshape(hs), fv.reshape(hs), fox_log_f)
        fox_out = fox_out.reshape(B, L, FOX_WIDTH).astype(h.dtype)

        lb = lower_bounds[l]
        one_minus_f = (1.0 - lb) * jax.nn.sigmoid(-hf.astype(jnp.float32))
        log_f = jnp.maximum(jnp.log1p(-one_minus_f), LOG_F_MIN)
        k_in = -jnp.expm1(log_f)
        q_h = jax.nn.silu(hq.astype(jnp.float32)) * (HG_EXPAND ** -0.5)
        ks_ = (B, L, HG_HEADS, HG_EXPAND)
        o_h = hgrn2_recurrence(q_h.reshape(ks_), k_in.reshape(ks_),
                               hi.astype(jnp.float32).reshape(B, L, HG_HEADS, HG_HEAD_V),
                               log_f.reshape(ks_))
        o_h = rms_norm(o_h, hgrn_norm_w[l])
        hgrn_out = (o_h.reshape(B, L, HG_V) * jax.nn.silu(hg.astype(jnp.float32))).astype(h.dtype)

        mixed = jnp.concatenate([fox_out, hgrn_out], axis=-1) @ w_out[l]
        h = h + mixed

        u = rms_norm(h, norm_ffn_w[l])
        h = h + (jax.nn.silu(u @ w_ffn_gate[l]) * (u @ w_ffn_up[l])) @ w_ffn_down[l]

    return rms_norm(h, norm_final_w)[:, N_META:]
```

```python
import functools

import numpy as np
import jax
import jax.numpy as jnp
from jax import lax
from jax.experimental import pallas as pl
from jax.experimental.pallas import tpu as pltpu

F32 = jnp.float32
BF16 = jnp.bfloat16

N_META = 16
FOX_HEADS = 8
FOX_HEAD_DIM = 64
FOX_WIDTH = FOX_HEADS * FOX_HEAD_DIM
HG_HEADS = 4
HG_K = 128
HG_V = 128
HG_WIDTH = HG_HEADS * HG_K
CHUNK = 64
EPS = 1e-6
MASK_VALUE = -1e30
LOG_F_MIN = -30.0

LANES = 128
SEQ_TILE = 256
ROW_TILE = 512
FF_TILE = 256
VMEM_LIMIT = 56 * 1024 * 1024

N_LEVELS = 6
N_EXP_ROWS = (2 + N_LEVELS) * CHUNK


def _rms(x, w):
    ms = jnp.mean(x * x, axis=-1, keepdims=True)
    return x * lax.rsqrt(ms + EPS) * w


def _split3(x):
    hi = x.astype(BF16)
    r1 = x - hi.astype(F32)
    mid = r1.astype(BF16)
    lo = (r1 - mid.astype(F32)).astype(BF16)
    return hi, mid, lo


def _silu(x):
    return x * jax.nn.sigmoid(x)


def _inproj_kernel(h_ref, nw_ref, w_ref, q_ref, k_ref, v_ref, hq_ref, hf_ref, hi_ref,
                   hg_ref, ff_ref):
    u = _rms(h_ref[...], nw_ref[...]).astype(BF16)

    def mm(c0, c1):
        return jnp.dot(u, w_ref[:, c0:c1], preferred_element_type=F32)

    w = FOX_WIDTH
    q_ref[...] = (mm(0, w) * (FOX_HEAD_DIM ** -0.5)).astype(BF16)
    k_ref[...] = mm(w, 2 * w).astype(BF16)
    v_ref[...] = mm(2 * w, 3 * w).astype(BF16)
    o = 3 * w
    hq_ref[...] = mm(o, o + HG_WIDTH)
    hf_ref[...] = mm(o + HG_WIDTH, o + 2 * HG_WIDTH)
    hi_ref[...] = mm(o + 2 * HG_WIDTH, o + 3 * HG_WIDTH).astype(BF16)
    hg_ref[...] = mm(o + 3 * HG_WIDTH, o + 4 * HG_WIDTH)
    ff_ref[...] = mm(o + 4 * HG_WIDTH, o + 4 * HG_WIDTH + LANES)


def _inproj(h, nw, w):
    rows, d = h.shape
    ncol = w.shape[1]
    grid = (rows // ROW_TILE,)
    row_spec = lambda c: pl.BlockSpec((ROW_TILE, c), lambda i: (i, 0))
    out_shape = (
        jax.ShapeDtypeStruct((rows, FOX_WIDTH), BF16),
        jax.ShapeDtypeStruct((rows, FOX_WIDTH), BF16),
        jax.ShapeDtypeStruct((rows, FOX_WIDTH), BF16),
        jax.ShapeDtypeStruct((rows, HG_WIDTH), F32),
        jax.ShapeDtypeStruct((rows, HG_WIDTH), F32),
        jax.ShapeDtypeStruct((rows, HG_WIDTH), BF16),
        jax.ShapeDtypeStruct((rows, HG_WIDTH), F32),
        jax.ShapeDtypeStruct((rows, LANES), F32),
    )
    return pl.pallas_call(
        _inproj_kernel,
        grid=grid,
        in_specs=[
            row_spec(d),
            pl.BlockSpec((1, d), lambda i: (0, 0)),
            pl.BlockSpec((d, ncol), lambda i: (0, 0), pipeline_mode=pl.Buffered(1)),
        ],
        out_specs=tuple(row_spec(s.shape[1]) for s in out_shape),
        out_shape=out_shape,
        compiler_params=pltpu.CompilerParams(
            dimension_semantics=("arbitrary",), vmem_limit_bytes=VMEM_LIMIT),
        name="inproj",
    )(h, nw, w)


def _gates_kernel(ff_ref, fb_ref, e_ref, *, padf):
    lp = ff_ref.shape[0]
    t = SEQ_TILE
    row = lax.broadcasted_iota(jnp.int32, (t, t), 0)
    col = lax.broadcasted_iota(jnp.int32, (t, t), 1)
    tri = (row >= col).astype(BF16)
    lane = lax.broadcasted_iota(jnp.int32, (t, LANES), 1)
    rloc = lax.broadcasted_iota(jnp.int32, (t, LANES), 0)
    carry = jnp.zeros((1, LANES), F32)
    for blk in range(lp // t):
        z = ff_ref[blk * t:(blk + 1) * t, :] + fb_ref[...]
        logf = jnp.minimum(z, 0.0) - jnp.log1p(jnp.exp(-jnp.abs(z)))
        hi, mid, lo = _split3(logf)
        c = (jnp.dot(tri, hi, preferred_element_type=F32)
             + jnp.dot(tri, mid, preferred_element_type=F32)
             + jnp.dot(tri, lo, preferred_element_type=F32)) + carry
        carry = c[t - 1:t, :]
        bias = jnp.where(rloc + blk * t >= padf, -c, MASK_VALUE)
        bh, bm, bl = _split3(bias)
        h8 = FOX_HEADS
        packed = jnp.where(
            lane < h8, bh.astype(F32),
            jnp.where(lane < 2 * h8, pltpu.roll(bm.astype(F32), h8, 1),
                      jnp.where(lane < 3 * h8, pltpu.roll(bl.astype(F32), 2 * h8, 1), 0.0)))
        e_ref[blk * t:(blk + 1) * t, :] = packed.astype(BF16)


def _gates(ff, fb, batch, lp, padf):
    return pl.pallas_call(
        functools.partial(_gates_kernel, padf=padf),
        grid=(batch,),
        in_specs=[pl.BlockSpec((lp, LANES), lambda b: (b, 0)),
                  pl.BlockSpec((1, LANES), lambda b: (0, 0))],
        out_specs=pl.BlockSpec((lp, LANES), lambda b: (b, 0)),
        out_shape=jax.ShapeDtypeStruct((batch * lp, LANES), BF16),
        compiler_params=pltpu.CompilerParams(
            dimension_semantics=("arbitrary",), vmem_limit_bytes=VMEM_LIMIT),
        name="fox_gates",
    )(ff, fb)


def _fox_kernel(q_ref, k_ref, v_ref, e_ref, o_ref):
    i = pl.program_id(1)
    t = SEQ_TILE
    half = FOX_HEAD_DIM
    lane = lax.broadcasted_iota(jnp.int32, (t, LANES), 1)
    row2 = lax.broadcasted_iota(jnp.int32, (2 * t, t), 0)
    col2 = lax.broadcasted_iota(jnp.int32, (2 * t, t), 1)
    causal = col2 <= jnp.where(row2 >= t, row2 - t, row2)
    nt = (((1,), (1,)), ((), ()))

    for p in range(FOX_HEADS // 2):
        sl = slice(p * LANES, (p + 1) * LANES)
        qp = q_ref[:, sl]
        zero = jnp.zeros_like(qp)
        ones_a = ((lane == 2 * p) | (lane == FOX_HEADS + 2 * p)
                  | (lane == 2 * FOX_HEADS + 2 * p)).astype(BF16)
        ones_b = ((lane == 2 * p + 1) | (lane == FOX_HEADS + 2 * p + 1)
                  | (lane == 2 * FOX_HEADS + 2 * p + 1)).astype(BF16)
        qa = jnp.concatenate([jnp.where(lane < half, qp, zero), ones_a], axis=1)
        qb = jnp.concatenate([jnp.where(lane >= half, qp, zero), ones_b], axis=1)
        qq = jnp.concatenate([qa, qb], axis=0)

        def step(j, carry, masked):
            m_old, l_old, acc = carry
            r0 = pl.multiple_of(j * t, t)
            kaug = jnp.concatenate([k_ref[pl.ds(r0, t), sl], e_ref[pl.ds(r0, t), :]], axis=1)
            s = lax.dot_general(qq, kaug, nt, preferred_element_type=F32)
            if masked:
                s = jnp.where(causal, s, MASK_VALUE)
            m_new = jnp.maximum(m_old, jnp.max(s, axis=1, keepdims=True))
            alpha = jnp.exp(m_old - m_new)
            pr = jnp.exp(s - m_new)
            l_new = alpha * l_old + jnp.sum(pr, axis=1, keepdims=True)
            pv = jnp.dot(pr.astype(BF16), v_ref[pl.ds(r0, t), sl], preferred_element_type=F32)
            return m_new, l_new, alpha * acc + pv

        init = (jnp.full((2 * t, 1), -jnp.inf, F32), jnp.zeros((2 * t, 1), F32),
                jnp.zeros((2 * t, LANES), F32))
        carry = lax.fori_loop(0, i, lambda j, c: step(j, c, False), init)
        _, l_fin, acc = step(i, carry, True)
        o = acc / l_fin
        o_ref[:, sl] = jnp.where(lane < half, o[:t], o[t:]).astype(o_ref.dtype)


def _fox(q, k, v, e, batch, lp):
    nq = lp // SEQ_TILE
    return pl.pallas_call(
        _fox_kernel,
        grid=(batch, nq),
        in_specs=[
            pl.BlockSpec((SEQ_TILE, FOX_WIDTH), lambda b, i: (b * nq + i, 0)),
            pl.BlockSpec((lp, FOX_WIDTH), lambda b, i: (b, 0)),
            pl.BlockSpec((lp, FOX_WIDTH), lambda b, i: (b, 0)),
            pl.BlockSpec((lp, LANES), lambda b, i: (b, 0)),
        ],
        out_specs=pl.BlockSpec((SEQ_TILE, FOX_WIDTH), lambda b, i: (b * nq + i, 0)),
        out_shape=jax.ShapeDtypeStruct((batch * lp, FOX_WIDTH), BF16),
        compiler_params=pltpu.CompilerParams(
            dimension_semantics=("arbitrary", "arbitrary"), vmem_limit_bytes=VMEM_LIMIT),
        name="fox_attention",
    )(q, k, v, e)


def _hgrn_constants():
    c = CHUNK
    m = np.zeros((N_EXP_ROWS, c), np.float32)
    upper = np.zeros((N_LEVELS, c), np.float32)
    for t in range(c):
        m[t, :t + 1] = 1.0
        m[c + t, t + 1:] = 1.0
        for lv in range(N_LEVELS):
            w = c >> (lv + 1)
            mid = (t // (2 * w)) * 2 * w + w
            if t >= mid:
                m[(2 + lv) * c + t, mid:t + 1] = 1.0
                upper[lv, t] = 1.0
            else:
                m[(2 + lv) * c + t, t + 1:mid] = 1.0
    level = np.full((c, c), N_LEVELS + 1, np.int32)
    for t in range(c):
        level[t, t] = N_LEVELS
        for s in range(t):
            w = 1 << int(np.floor(np.log2(t ^ s)))
            level[t, s] = int(np.log2(c // (2 * w)))
    m3 = np.concatenate([m, m, m], axis=1)
    return m3, upper, level


def _hgrn_kernel(hq_ref, hf_ref, hi_ref, hg_ref, lbraw_ref, nw_ref, m3_ref, up_ref, lvl_ref,
                 o_ref, st_ref, *, layer):
    @pl.when(pl.program_id(1) == 0)
    def _():
        st_ref[...] = jnp.zeros_like(st_ref)

    c = CHUNK
    raw = lbraw_ref[...]
    ex = jnp.exp(raw - jnp.max(raw, axis=0, keepdims=True))
    s_lb = ex / jnp.sum(ex, axis=0, keepdims=True)
    lb = jnp.sum(s_lb[:layer + 1], axis=0, keepdims=True) - s_lb[0:1]
    level = lvl_ref[...]
    nt = (((1,), (1,)), ((), ()))
    tn = (((0,), (0,)), ((), ()))

    def chunk_body(ci, _):
        r0 = pl.multiple_of(ci * c, c)
        rows = pl.ds(r0, c)
        z = hf_ref[rows, :]
        omf = (1.0 - lb) * jax.nn.sigmoid(-z)
        l1p = jnp.log1p(-omf)
        g = jnp.maximum(l1p, LOG_F_MIN)
        kk = jnp.where(l1p >= LOG_F_MIN, omf, 1.0 - np.exp(LOG_F_MIN))
        q = _silu(hq_ref[rows, :]) * (HG_K ** -0.5)
        v = hi_ref[rows, :]
        ghi, gmid, glo = _split3(g)
        g3 = jnp.concatenate([ghi, gmid, glo], axis=0)
        e_all = jnp.dot(m3_ref[...], g3, preferred_element_type=F32)
        b = e_all[0:c]
        qd = (q * jnp.exp(b)).astype(BF16)
        kd = (kk * jnp.exp(e_all[c:2 * c])).astype(BF16)
        dec_last = jnp.exp(b[c - 1:c, :])
        ql, kl = [], []
        for lv in range(N_LEVELS):
            x = jnp.exp(e_all[(2 + lv) * c:(3 + lv) * c])
            up = up_ref[lv] > 0.5
            ql.append(jnp.where(up, q * x, 0.0).astype(BF16))
            kl.append(jnp.where(up, 0.0, kk * x).astype(BF16))
        qb = q.astype(BF16)
        kb = kk.astype(BF16)
        gate = _silu(hg_ref[rows, :])
        for h in range(HG_HEADS):
            hs = slice(h * HG_K, (h + 1) * HG_K)
            a = jnp.where(level == N_LEVELS,
                          lax.dot_general(qb[:, hs], kb[:, hs], nt, preferred_element_type=F32),
                          0.0)
            for lv in range(N_LEVELS):
                pl_ = lax.dot_general(ql[lv][:, hs], kl[lv][:, hs], nt,
                                      preferred_element_type=F32)
                a = jnp.where(level == lv, pl_, a)
            st = st_ref[h]
            vh = v[:, hs]
            o = (lax.dot_general(qd[:, hs], st.astype(BF16), nt, preferred_element_type=F32)
                 + jnp.dot(a.astype(BF16), vh, preferred_element_type=F32))
            st_ref[h] = st * dec_last[:, hs] + lax.dot_general(
                vh, kd[:, hs], tn, preferred_element_type=F32)
            on = _rms(o, nw_ref[...])
            o_ref[rows, hs] = (on * gate[:, hs]).astype(o_ref.dtype)
        return 0

    lax.fori_loop(0, hq_ref.shape[0] // c, chunk_body, 0)


def _hgrn(hq, hf, hi, hg, lb_raw, nw, layer, batch, lp):
    nb = lp // SEQ_TILE
    m3, upper, level = _hgrn_constants()
    row_spec = pl.BlockSpec((SEQ_TILE, HG_WIDTH), lambda b, i: (b * nb + i, 0))
    full = lambda shape: pl.BlockSpec(shape, lambda b, i: (0,) * len(shape))
    depth = lb_raw.shape[0]
    return pl.pallas_call(
        functools.partial(_hgrn_kernel, layer=layer),
        grid=(batch, nb),
        in_specs=[row_spec, row_spec, row_spec, row_spec,
                  full((depth, HG_WIDTH)), full((1, HG_V)),
                  full(m3.shape), full((N_LEVELS, CHUNK, 1)), full(level.shape)],
        out_specs=row_spec,
        out_shape=jax.ShapeDtypeStruct((batch * lp, HG_WIDTH), BF16),
        scratch_shapes=[pltpu.VMEM((HG_HEADS, HG_V, HG_K), F32)],
        compiler_params=pltpu.CompilerParams(
            dimension_semantics=("arbitrary", "arbitrary"), vmem_limit_bytes=VMEM_LIMIT),
        name="hgrn2",
    )(hq, hf, hi, hg, lb_raw, nw, jnp.asarray(m3, BF16),
      jnp.asarray(upper[:, :, None], F32), jnp.asarray(level))


def _mlp_kernel(h_ref, fo_ref, ho_ref, wo_ref, nw_ref, wg_ref, wu_ref, wd_ref, fnw_ref,
                out_ref, a_ref, *, final):
    mixed = (jnp.dot(fo_ref[...], wo_ref[0:FOX_WIDTH, :], preferred_element_type=F32)
             + jnp.dot(ho_ref[...], wo_ref[FOX_WIDTH:, :], preferred_element_type=F32))
    h1 = h_ref[...] + mixed
    u = _rms(h1, nw_ref[...]).astype(BF16)
    dff = wg_ref.shape[1]
    for c0 in range(0, dff, FF_TILE):
        cs = slice(c0, c0 + FF_TILE)
        gate = jnp.dot(u, wg_ref[:, cs], preferred_element_type=F32)
        up = jnp.dot(u, wu_ref[:, cs], preferred_element_type=F32)
        a_ref[:, cs] = (_silu(gate) * up).astype(BF16)
    h2 = h1 + jnp.dot(a_ref[...], wd_ref[...], preferred_element_type=F32)
    if final:
        h2 = _rms(h2, fnw_ref[...])
    out_ref[...] = h2


def _mlp(h, fo, ho, wo, nw, wg, wu, wd, fnw, final):
    rows, d = h.shape
    dff = wg.shape[1]
    row_spec = lambda c: pl.BlockSpec((ROW_TILE, c), lambda i: (i, 0))
    const = lambda shape: pl.BlockSpec(shape, lambda i: (0, 0), pipeline_mode=pl.Buffered(1))
    return pl.pallas_call(
        functools.partial(_mlp_kernel, final=final),
        grid=(rows // ROW_TILE,),
        in_specs=[row_spec(d), row_spec(FOX_WIDTH), row_spec(HG_WIDTH),
                  const(wo.shape), pl.BlockSpec((1, d), lambda i: (0, 0)),
                  const(wg.shape), const(wu.shape), const(wd.shape),
                  pl.BlockSpec((1, d), lambda i: (0, 0))],
        out_specs=row_spec(d),
        out_shape=jax.ShapeDtypeStruct((rows, d), F32),
        scratch_shapes=[pltpu.VMEM((ROW_TILE, dff), BF16)],
        compiler_params=pltpu.CompilerParams(
            dimension_semantics=("arbitrary",), vmem_limit_bytes=VMEM_LIMIT),
        name="mlp_final" if final else "mlp",
    )(h, fo, ho, wo, nw, wg, wu, wd, fnw)


def kernel(x, meta, norm_mix_w, w_in, fox_f_bias, hgrn_lb_raw, hgrn_norm_w, w_out,
           norm_ffn_w, w_ffn_gate, w_ffn_up, w_ffn_down, norm_final_w):
    batch, seq, d = x.shape
    depth = w_in.shape[0]
    real = N_META + seq
    lp = -(-real // SEQ_TILE) * SEQ_TILE
    while (batch * lp) % ROW_TILE:
        lp += SEQ_TILE
    padf = lp - real

    head = jnp.concatenate([jnp.zeros((padf, d), x.dtype), meta.astype(x.dtype)], axis=0)
    h = jnp.concatenate([jnp.broadcast_to(head[None], (batch, padf + N_META, d)), x], axis=1)
    h = h.reshape(batch * lp, d)

    fw = 3 * FOX_WIDTH
    lb_raw = hgrn_lb_raw.astype(F32)
    for l in range(depth):
        wl = w_in[l]
        w_cat = jnp.concatenate(
            [wl[:, :fw], wl[:, fw + FOX_HEADS:], wl[:, fw:fw + FOX_HEADS],
             jnp.zeros((d, LANES - FOX_HEADS), wl.dtype)], axis=1).astype(BF16)
        fb = jnp.concatenate([fox_f_bias[l].astype(F32),
                              jnp.zeros((LANES - FOX_HEADS,), F32)])[None]
        q, k, v, hq, hf, hi, hg, ff = _inproj(h, norm_mix_w[l][None].astype(F32), w_cat)
        e = _gates(ff, fb, batch, lp, padf)
        fox_o = _fox(q, k, v, e, batch, lp)
        hgrn_o = _hgrn(hq, hf, hi, hg, lb_raw, hgrn_norm_w[l][None].astype(F32), l, batch, lp)
        h = _mlp(h, fox_o, hgrn_o, w_out[l].astype(BF16), norm_ffn_w[l][None].astype(F32),
                 w_ffn_gate[l].astype(BF16), w_ffn_up[l].astype(BF16),
                 w_ffn_down[l].astype(BF16), norm_final_w[None].astype(F32),
                 final=(l == depth - 1))
    return h.reshape(batch, lp, d)[:, padf + N_META:]
```

```python
import functools

import numpy as np
import jax
import jax.numpy as jnp
from jax import lax
from jax.experimental import pallas as pl
from jax.experimental.pallas import tpu as pltpu

F32 = jnp.float32
BF16 = jnp.bfloat16

N_META = 16
FOX_HEADS = 8
FOX_HEAD_DIM = 64
FOX_WIDTH = FOX_HEADS * FOX_HEAD_DIM
HG_HEADS = 4
HG_K = 128
HG_V = 128
HG_WIDTH = HG_HEADS * HG_K
CHUNK = 64
EPS = 1e-6
MASK_VALUE = -1e30
LOG_F_MIN = -30.0
LOG2E = 1.4426950408889634

LANES = 128
SEQ_TILE = 256
ROW_TILE = 512
FF_TILE = 256
VMEM_LIMIT = 56 * 1024 * 1024

N_LEVELS = 6
N_EXP_ROWS = (2 + N_LEVELS) * CHUNK


def _rms(x, w):
    ms = jnp.mean(x * x, axis=-1, keepdims=True)
    return x * lax.rsqrt(ms + EPS) * w


def _split3(x):
    hi = x.astype(BF16)
    r1 = x - hi.astype(F32)
    mid = r1.astype(BF16)
    lo = (r1 - mid.astype(F32)).astype(BF16)
    return hi, mid, lo


def _silu(x):
    return x * jax.nn.sigmoid(x)


def _inproj_kernel(h_ref, nw_ref, w_ref, qt_ref, k_ref, vt_ref, hq_ref, hf_ref, hi_ref,
                   hg_ref, ff_ref):
    u = _rms(h_ref[...], nw_ref[...]).astype(BF16)

    def mm(c0, c1):
        return jnp.dot(u, w_ref[:, c0:c1], preferred_element_type=F32)

    w = FOX_WIDTH
    q = mm(0, w) * (FOX_HEAD_DIM ** -0.5 * LOG2E)
    v = mm(2 * w, 3 * w)
    for s in range(ROW_TILE // SEQ_TILE):
        rows = slice(s * SEQ_TILE, (s + 1) * SEQ_TILE)
        qt_ref[s] = q[rows].T.astype(BF16)
        vt_ref[s] = v[rows].T.astype(BF16)
    k_ref[...] = mm(w, 2 * w).astype(BF16)
    o = 3 * w
    hq_ref[...] = mm(o, o + HG_WIDTH)
    hf_ref[...] = mm(o + HG_WIDTH, o + 2 * HG_WIDTH)
    hi_ref[...] = mm(o + 2 * HG_WIDTH, o + 3 * HG_WIDTH).astype(BF16)
    hg_ref[...] = mm(o + 3 * HG_WIDTH, o + 4 * HG_WIDTH)
    ff_ref[...] = mm(o + 4 * HG_WIDTH, o + 4 * HG_WIDTH + LANES)


def _inproj(h, nw, w):
    rows, d = h.shape
    ncol = w.shape[1]
    grid = (rows // ROW_TILE,)
    row_spec = lambda c: pl.BlockSpec((ROW_TILE, c), lambda i: (i, 0))
    slabs = ROW_TILE // SEQ_TILE
    slab_shape = jax.ShapeDtypeStruct((rows // SEQ_TILE, FOX_WIDTH, SEQ_TILE), BF16)
    slab_spec = pl.BlockSpec((slabs, FOX_WIDTH, SEQ_TILE), lambda i: (i, 0, 0))
    out_shape = (
        slab_shape,
        jax.ShapeDtypeStruct((rows, FOX_WIDTH), BF16),
        slab_shape,
        jax.ShapeDtypeStruct((rows, HG_WIDTH), F32),
        jax.ShapeDtypeStruct((rows, HG_WIDTH), F32),
        jax.ShapeDtypeStruct((rows, HG_WIDTH), BF16),
        jax.ShapeDtypeStruct((rows, HG_WIDTH), F32),
        jax.ShapeDtypeStruct((rows, LANES), F32),
    )
    return pl.pallas_call(
        _inproj_kernel,
        grid=grid,
        in_specs=[
            row_spec(d),
            pl.BlockSpec((1, d), lambda i: (0, 0)),
            pl.BlockSpec((d, ncol), lambda i: (0, 0), pipeline_mode=pl.Buffered(1)),
        ],
        out_specs=tuple(slab_spec if len(s.shape) == 3 else row_spec(s.shape[1])
                        for s in out_shape),
        out_shape=out_shape,
        compiler_params=pltpu.CompilerParams(
            dimension_semantics=("arbitrary",), vmem_limit_bytes=VMEM_LIMIT),
        name="inproj",
    )(h, nw, w)


def _gates_kernel(ff_ref, fb_ref, e_ref, *, padf):
    lp = ff_ref.shape[0]
    t = SEQ_TILE
    row = lax.broadcasted_iota(jnp.int32, (t, t), 0)
    col = lax.broadcasted_iota(jnp.int32, (t, t), 1)
    tri = (row >= col).astype(BF16)
    lane = lax.broadcasted_iota(jnp.int32, (t, LANES), 1)
    rloc = lax.broadcasted_iota(jnp.int32, (t, LANES), 0)
    carry = jnp.zeros((1, LANES), F32)
    for blk in range(lp // t):
        z = ff_ref[blk * t:(blk + 1) * t, :] + fb_ref[...]
        logf = jnp.minimum(z, 0.0) - jnp.log1p(jnp.exp(-jnp.abs(z)))
        hi, mid, lo = _split3(logf)
        c = (jnp.dot(tri, hi, preferred_element_type=F32)
             + jnp.dot(tri, mid, preferred_element_type=F32)
             + jnp.dot(tri, lo, preferred_element_type=F32)) + carry
        carry = c[t - 1:t, :]
        bias = jnp.where(rloc + blk * t >= padf, -LOG2E * c, MASK_VALUE)
        bh, bm, bl = _split3(bias)
        h8 = FOX_HEADS
        packed = jnp.where(
            lane < h8, bh.astype(F32),
            jnp.where(lane < 2 * h8, pltpu.roll(bm.astype(F32), h8, 1),
                      jnp.where(lane < 3 * h8, pltpu.roll(bl.astype(F32), 2 * h8, 1), 0.0)))
        e_ref[blk * t:(blk + 1) * t, :] = packed.astype(BF16)


def _gates(ff, fb, batch, lp, padf):
    return pl.pallas_call(
        functools.partial(_gates_kernel, padf=padf),
        grid=(batch,),
        in_specs=[pl.BlockSpec((lp, LANES), lambda b: (b, 0)),
                  pl.BlockSpec((1, LANES), lambda b: (0, 0))],
        out_specs=pl.BlockSpec((lp, LANES), lambda b: (b, 0)),
        out_shape=jax.ShapeDtypeStruct((batch * lp, LANES), BF16),
        compiler_params=pltpu.CompilerParams(
            dimension_semantics=("arbitrary",), vmem_limit_bytes=VMEM_LIMIT),
        name="fox_gates",
    )(ff, fb)


def _fox_kernel(qt_ref, k_ref, vt_ref, e_ref, o_ref):
    i = pl.program_id(1)
    t = SEQ_TILE
    half = FOX_HEAD_DIM
    npair = FOX_HEADS // 2
    n_one = 16
    srow = lax.broadcasted_iota(jnp.int32, (LANES, t), 0)
    key_i = lax.broadcasted_iota(jnp.int32, (t, 2 * t), 0)
    qry_i = lax.broadcasted_iota(jnp.int32, (t, 2 * t), 1)
    causal = key_i <= jnp.where(qry_i >= t, qry_i - t, qry_i)
    ones_rows = jnp.ones((n_one, t), BF16)
    lanes_of = lambda p: slice(p * LANES, (p + 1) * LANES)

    def stacked_qt(p):
        top = qt_ref[0, lanes_of(p), :]
        zero = jnp.zeros_like(top)
        sel_a = ((srow == 2 * p) | (srow == FOX_HEADS + 2 * p)
                 | (srow == 2 * FOX_HEADS + 2 * p)).astype(BF16)
        sel_b = ((srow == 2 * p + 1) | (srow == FOX_HEADS + 2 * p + 1)
                 | (srow == 2 * FOX_HEADS + 2 * p + 1)).astype(BF16)
        col_a = jnp.concatenate([jnp.where(srow < half, top, zero), sel_a], axis=0)
        col_b = jnp.concatenate([jnp.where(srow >= half, top, zero), sel_b], axis=0)
        return jnp.concatenate([col_a, col_b], axis=1)

    qqt = [stacked_qt(p) for p in range(npair)]

    def step(j, carry, masked):
        r0 = pl.multiple_of(j * t, t)
        ebias = e_ref[pl.ds(r0, t), :]
        scores = []
        for p in range(npair):
            kaug = jnp.concatenate([k_ref[pl.ds(r0, t), lanes_of(p)], ebias], axis=1)
            st = jnp.dot(kaug, qqt[p], preferred_element_type=F32)
            scores.append(jnp.where(causal, st, MASK_VALUE) if masked else st)
        probs = []
        for p in range(npair):
            m_old = carry[p][0]
            m_new = jnp.maximum(m_old, jnp.max(scores[p], axis=0, keepdims=True))
            probs.append((m_new, jnp.exp2(m_old - m_new),
                          jnp.exp2(scores[p] - m_new).astype(BF16)))
        out = []
        for p in range(npair):
            m_new, alpha, pr = probs[p]
            vaug = jnp.concatenate([vt_ref[j, lanes_of(p), :], ones_rows], axis=0)
            pv = jnp.dot(vaug, pr, preferred_element_type=F32)
            out.append((m_new, alpha * carry[p][1] + pv))
        return tuple(out)

    init = tuple((jnp.full((1, 2 * t), -jnp.inf, F32), jnp.zeros((LANES + n_one, 2 * t), F32))
                 for _ in range(npair))
    carry = lax.fori_loop(0, i, lambda j, c: step(j, c, False), init)
    fin = step(i, carry, True)
    for p in range(npair):
        acc = fin[p][1]
        ot = acc[0:LANES] / acc[LANES:LANES + 1]
        pair_t = jnp.concatenate([ot[0:half, 0:t], ot[half:LANES, t:2 * t]], axis=0)
        o_ref[:, lanes_of(p)] = pair_t.T.astype(o_ref.dtype)


def _fox(qt, k, vt, e, batch, lp):
    nq = lp // SEQ_TILE
    return pl.pallas_call(
        _fox_kernel,
        grid=(batch, nq),
        in_specs=[
            pl.BlockSpec((1, FOX_WIDTH, SEQ_TILE), lambda b, i: (b * nq + i, 0, 0)),
            pl.BlockSpec((lp, FOX_WIDTH), lambda b, i: (b, 0)),
            pl.BlockSpec((nq, FOX_WIDTH, SEQ_TILE), lambda b, i: (b, 0, 0)),
            pl.BlockSpec((lp, LANES), lambda b, i: (b, 0)),
        ],
        out_specs=pl.BlockSpec((SEQ_TILE, FOX_WIDTH), lambda b, i: (b * nq + i, 0)),
        out_shape=jax.ShapeDtypeStruct((batch * lp, FOX_WIDTH), BF16),
        compiler_params=pltpu.CompilerParams(
            dimension_semantics=("arbitrary", "arbitrary"), vmem_limit_bytes=VMEM_LIMIT),
        name="fox_attention",
    )(qt, k, vt, e)


def _hgrn_constants():
    c = CHUNK
    m = np.zeros((N_EXP_ROWS, c), np.float32)
    upper = np.zeros((N_LEVELS, c), np.float32)
    for t in range(c):
        m[t, :t + 1] = 1.0
        m[c + t, t + 1:] = 1.0
        for lv in range(N_LEVELS):
            w = c >> (lv + 1)
            mid = (t // (2 * w)) * 2 * w + w
            if t >= mid:
                m[(2 + lv) * c + t, mid:t + 1] = 1.0
                upper[lv, t] = 1.0
            else:
                m[(2 + lv) * c + t, t + 1:mid] = 1.0
    level = np.full((c, c), N_LEVELS + 1, np.int32)
    for t in range(c):
        level[t, t] = N_LEVELS
        for s in range(t):
            w = 1 << int(np.floor(np.log2(t ^ s)))
            level[t, s] = int(np.log2(c // (2 * w)))
    m3 = np.concatenate([m, m, m], axis=1)
    return m3, upper, level


def _hgrn_kernel(hq_ref, hf_ref, hi_ref, hg_ref, lbraw_ref, nw_ref, m3_ref, up_ref, lvl_ref,
                 o_ref, st_ref, *, layer):
    @pl.when(pl.program_id(1) == 0)
    def _():
        st_ref[...] = jnp.zeros_like(st_ref)

    c = CHUNK
    raw = lbraw_ref[...]
    ex = jnp.exp(raw - jnp.max(raw, axis=0, keepdims=True))
    s_lb = ex / jnp.sum(ex, axis=0, keepdims=True)
    lb = jnp.sum(s_lb[:layer + 1], axis=0, keepdims=True) - s_lb[0:1]
    level = lvl_ref[...]
    nt = (((1,), (1,)), ((), ()))
    tn = (((0,), (0,)), ((), ()))

    def chunk_body(ci, _):
        r0 = pl.multiple_of(ci * c, c)
        rows = pl.ds(r0, c)
        z = hf_ref[rows, :]
        omf = (1.0 - lb) * jax.nn.sigmoid(-z)
        l1p = jnp.log1p(-omf)
        g = jnp.maximum(l1p, LOG_F_MIN)
        kk = jnp.where(l1p >= LOG_F_MIN, omf, 1.0 - np.exp(LOG_F_MIN))
        q = _silu(hq_ref[rows, :]) * (HG_K ** -0.5)
        v = hi_ref[rows, :]
        ghi, gmid, glo = _split3(g)
        g3 = jnp.concatenate([ghi, gmid, glo], axis=0)
        e_all = jnp.dot(m3_ref[...], g3, preferred_element_type=F32)
        b = e_all[0:c]
        qd = (q * jnp.exp(b)).astype(BF16)
        kd = (kk * jnp.exp(e_all[c:2 * c])).astype(BF16)
        dec_last = jnp.exp(b[c - 1:c, :])
        ql, kl = [], []
        for lv in range(N_LEVELS):
            x = jnp.exp(e_all[(2 + lv) * c:(3 + lv) * c])
            up = up_ref[lv] > 0.5
            ql.append(jnp.where(up, q * x, 0.0).astype(BF16))
            kl.append(jnp.where(up, 0.0, kk * x).astype(BF16))
        qb = q.astype(BF16)
        kb = kk.astype(BF16)
        gate = _silu(hg_ref[rows, :])
        for h in range(HG_HEADS):
            hs = slice(h * HG_K, (h + 1) * HG_K)
            a = jnp.where(level == N_LEVELS,
                          lax.dot_general(qb[:, hs], kb[:, hs], nt, preferred_element_type=F32),
                          0.0)
            for lv in range(N_LEVELS):
                pl_ = lax.dot_general(ql[lv][:, hs], kl[lv][:, hs], nt,
                                      preferred_element_type=F32)
                a = jnp.where(level == lv, pl_, a)
            st = st_ref[h]
            vh = v[:, hs]
            o = (lax.dot_general(qd[:, hs], st.astype(BF16), nt, preferred_element_type=F32)
                 + jnp.dot(a.astype(BF16), vh, preferred_element_type=F32))
            st_ref[h] = st * dec_last[:, hs] + lax.dot_general(
                vh, kd[:, hs], tn, preferred_element_type=F32)
            on = _rms(o, nw_ref[...])
            o_ref[rows, hs] = (on * gate[:, hs]).astype(o_ref.dtype)
        return 0

    lax.fori_loop(0, hq_ref.shape[0] // c, chunk_body, 0)


def _hgrn(hq, hf, hi, hg, lb_raw, nw, layer, batch, lp):
    nb = lp // SEQ_TILE
    m3, upper, level = _hgrn_constants()
    row_spec = pl.BlockSpec((SEQ_TILE, HG_WIDTH), lambda b, i: (b * nb + i, 0))
    full = lambda shape: pl.BlockSpec(shape, lambda b, i: (0,) * len(shape))
    depth = lb_raw.shape[0]
    return pl.pallas_call(
        functools.partial(_hgrn_kernel, layer=layer),
        grid=(batch, nb),
        in_specs=[row_spec, row_spec, row_spec, row_spec,
                  full((depth, HG_WIDTH)), full((1, HG_V)),
                  full(m3.shape), full((N_LEVELS, CHUNK, 1)), full(level.shape)],
        out_specs=row_spec,
        out_shape=jax.ShapeDtypeStruct((batch * lp, HG_WIDTH), BF16),
        scratch_shapes=[pltpu.VMEM((HG_HEADS, HG_V, HG_K), F32)],
        compiler_params=pltpu.CompilerParams(
            dimension_semantics=("arbitrary", "arbitrary"), vmem_limit_bytes=VMEM_LIMIT),
        name="hgrn2",
    )(hq, hf, hi, hg, lb_raw, nw, jnp.asarray(m3, BF16),
      jnp.asarray(upper[:, :, None], F32), jnp.asarray(level))


def _mlp_kernel(h_ref, fo_ref, ho_ref, wo_ref, nw_ref, wg_ref, wu_ref, wd_ref, fnw_ref,
                out_ref, a_ref, *, final):
    mixed = (jnp.dot(fo_ref[...], wo_ref[0:FOX_WIDTH, :], preferred_element_type=F32)
             + jnp.dot(ho_ref[...], wo_ref[FOX_WIDTH:, :], preferred_element_type=F32))
    h1 = h_ref[...] + mixed
    u = _rms(h1, nw_ref[...]).astype(BF16)
    dff = wg_ref.shape[1]
    for c0 in range(0, dff, FF_TILE):
        cs = slice(c0, c0 + FF_TILE)
        gate = jnp.dot(u, wg_ref[:, cs], preferred_element_type=F32)
        up = jnp.dot(u, wu_ref[:, cs], preferred_element_type=F32)
        a_ref[:, cs] = (_silu(gate) * up).astype(BF16)
    h2 = h1 + jnp.dot(a_ref[...], wd_ref[...], preferred_element_type=F32)
    if final:
        h2 = _rms(h2, fnw_ref[...])
    out_ref[...] = h2


def _mlp(h, fo, ho, wo, nw, wg, wu, wd, fnw, final):
    rows, d = h.shape
    dff = wg.shape[1]
    row_spec = lambda c: pl.BlockSpec((ROW_TILE, c), lambda i: (i, 0))
    const = lambda shape: pl.BlockSpec(shape, lambda i: (0, 0), pipeline_mode=pl.Buffered(1))
    return pl.pallas_call(
        functools.partial(_mlp_kernel, final=final),
        grid=(rows // ROW_TILE,),
        in_specs=[row_spec(d), row_spec(FOX_WIDTH), row_spec(HG_WIDTH),
                  const(wo.shape), pl.BlockSpec((1, d), lambda i: (0, 0)),
                  const(wg.shape), const(wu.shape), const(wd.shape),
                  pl.BlockSpec((1, d), lambda i: (0, 0))],
        out_specs=row_spec(d),
        out_shape=jax.ShapeDtypeStruct((rows, d), F32),
        scratch_shapes=[pltpu.VMEM((ROW_TILE, dff), BF16)],
        compiler_params=pltpu.CompilerParams(
            dimension_semantics=("arbitrary",), vmem_limit_bytes=VMEM_LIMIT),
        name="mlp_final" if final else "mlp",
    )(h, fo, ho, wo, nw, wg, wu, wd, fnw)


def kernel(x, meta, norm_mix_w, w_in, fox_f_bias, hgrn_lb_raw, hgrn_norm_w, w_out,
           norm_ffn_w, w_ffn_gate, w_ffn_up, w_ffn_down, norm_final_w):
    batch, seq, d = x.shape
    depth = w_in.shape[0]
    real = N_META + seq
    lp = -(-real // SEQ_TILE) * SEQ_TILE
    while (batch * lp) % ROW_TILE:
        lp += SEQ_TILE
    padf = lp - real

    head = jnp.concatenate([jnp.zeros((padf, d), x.dtype), meta.astype(x.dtype)], axis=0)
    h = jnp.concatenate([jnp.broadcast_to(head[None], (batch, padf + N_META, d)), x], axis=1)
    h = h.reshape(batch * lp, d)

    fw = 3 * FOX_WIDTH
    lb_raw = hgrn_lb_raw.astype(F32)
    for l in range(depth):
        wl = w_in[l]
        w_cat = jnp.concatenate(
            [wl[:, :fw], wl[:, fw + FOX_HEADS:], wl[:, fw:fw + FOX_HEADS],
             jnp.zeros((d, LANES - FOX_HEADS), wl.dtype)], axis=1).astype(BF16)
        fb = jnp.concatenate([fox_f_bias[l].astype(F32),
                              jnp.zeros((LANES - FOX_HEADS,), F32)])[None]
        qt, k, vt, hq, hf, hi, hg, ff = _inproj(h, norm_mix_w[l][None].astype(F32), w_cat)
        e = _gates(ff, fb, batch, lp, padf)
        fox_o = _fox(qt, k, vt, e, batch, lp)
        hgrn_o = _hgrn(hq, hf, hi, hg, lb_raw, hgrn_norm_w[l][None].astype(F32), l, batch, lp)
        h = _mlp(h, fox_o, hgrn_o, w_out[l].astype(BF16), norm_ffn_w[l][None].astype(F32),
                 w_ffn_gate[l].astype(BF16), w_ffn_up[l].astype(BF16),
                 w_ffn_down[l].astype(BF16), norm_final_w[None].astype(F32),
                 final=(l == depth - 1))
    return h.reshape(batch, lp, d)[:, padf + N_META:]
```

```python
import functools

import numpy as np
import jax
import jax.numpy as jnp
from jax import lax
from jax.experimental import pallas as pl
from jax.experimental.pallas import tpu as pltpu

F32 = jnp.float32
BF16 = jnp.bfloat16

N_META = 16
FOX_HEADS = 8
FOX_HEAD_DIM = 64
FOX_WIDTH = FOX_HEADS * FOX_HEAD_DIM
HG_HEADS = 4
HG_K = 128
HG_V = 128
HG_WIDTH = HG_HEADS * HG_K
CHUNK = 64
EPS = 1e-6
MASK_VALUE = -1e30
LOG_F_MIN = -30.0
LOG2E = 1.4426950408889634

LANES = 128
SEQ_TILE = 256
ROW_TILE = 512
FF_TILE = 256
VMEM_LIMIT = 56 * 1024 * 1024

N_LEVELS = 6
N_EXP_ROWS = (2 + N_LEVELS) * CHUNK


def _rms(x, w):
    ms = jnp.mean(x * x, axis=-1, keepdims=True)
    return x * lax.rsqrt(ms + EPS) * w


def _split3(x):
    hi = x.astype(BF16)
    r1 = x - hi.astype(F32)
    mid = r1.astype(BF16)
    lo = (r1 - mid.astype(F32)).astype(BF16)
    return hi, mid, lo


def _silu(x, scale=1.0):
    return (scale * x) / (1.0 + jnp.exp2(-LOG2E * x))


def _inproj_kernel(h_ref, nw_ref, w_ref, qt_ref, k_ref, vt_ref, hq_ref, hf_ref, hi_ref,
                   hg_ref, ff_ref):
    u = _rms(h_ref[...], nw_ref[...]).astype(BF16)

    def mm(c0, c1):
        return jnp.dot(u, w_ref[:, c0:c1], preferred_element_type=F32)

    w = FOX_WIDTH
    q = mm(0, w) * (FOX_HEAD_DIM ** -0.5 * LOG2E)
    v = mm(2 * w, 3 * w)
    for s in range(ROW_TILE // SEQ_TILE):
        rows = slice(s * SEQ_TILE, (s + 1) * SEQ_TILE)
        qt_ref[s] = q[rows].T.astype(BF16)
        vt_ref[s] = v[rows].T.astype(BF16)
    k_ref[...] = mm(w, 2 * w).astype(BF16)
    o = 3 * w
    hq_ref[...] = mm(o, o + HG_WIDTH)
    hf_ref[...] = mm(o + HG_WIDTH, o + 2 * HG_WIDTH)
    hi_ref[...] = mm(o + 2 * HG_WIDTH, o + 3 * HG_WIDTH).astype(BF16)
    hg_ref[...] = mm(o + 3 * HG_WIDTH, o + 4 * HG_WIDTH)
    ff_ref[...] = mm(o + 4 * HG_WIDTH, o + 4 * HG_WIDTH + LANES)


def _inproj(h, nw, w):
    rows, d = h.shape
    ncol = w.shape[1]
    grid = (rows // ROW_TILE,)
    row_spec = lambda c: pl.BlockSpec((ROW_TILE, c), lambda i: (i, 0))
    slabs = ROW_TILE // SEQ_TILE
    slab_shape = jax.ShapeDtypeStruct((rows // SEQ_TILE, FOX_WIDTH, SEQ_TILE), BF16)
    slab_spec = pl.BlockSpec((slabs, FOX_WIDTH, SEQ_TILE), lambda i: (i, 0, 0))
    out_shape = (
        slab_shape,
        jax.ShapeDtypeStruct((rows, FOX_WIDTH), BF16),
        slab_shape,
        jax.ShapeDtypeStruct((rows, HG_WIDTH), F32),
        jax.ShapeDtypeStruct((rows, HG_WIDTH), F32),
        jax.ShapeDtypeStruct((rows, HG_WIDTH), BF16),
        jax.ShapeDtypeStruct((rows, HG_WIDTH), F32),
        jax.ShapeDtypeStruct((rows, LANES), F32),
    )
    return pl.pallas_call(
        _inproj_kernel,
        grid=grid,
        in_specs=[
            row_spec(d),
            pl.BlockSpec((1, d), lambda i: (0, 0)),
            pl.BlockSpec((d, ncol), lambda i: (0, 0), pipeline_mode=pl.Buffered(1)),
        ],
        out_specs=tuple(slab_spec if len(s.shape) == 3 else row_spec(s.shape[1])
                        for s in out_shape),
        out_shape=out_shape,
        compiler_params=pltpu.CompilerParams(
            dimension_semantics=("arbitrary",), vmem_limit_bytes=VMEM_LIMIT),
        name="inproj",
    )(h, nw, w)


def _gates_kernel(ff_ref, fb_ref, e_ref, *, padf):
    lp = ff_ref.shape[0]
    t = SEQ_TILE
    row = lax.broadcasted_iota(jnp.int32, (t, t), 0)
    col = lax.broadcasted_iota(jnp.int32, (t, t), 1)
    tri = (row >= col).astype(BF16)
    lane = lax.broadcasted_iota(jnp.int32, (t, LANES), 1)
    rloc = lax.broadcasted_iota(jnp.int32, (t, LANES), 0)
    carry = jnp.zeros((1, LANES), F32)
    for blk in range(lp // t):
        z = ff_ref[blk * t:(blk + 1) * t, :] + fb_ref[...]
        logf = jnp.minimum(z, 0.0) - jnp.log1p(jnp.exp(-jnp.abs(z)))
        hi, mid, lo = _split3(logf)
        c = (jnp.dot(tri, hi, preferred_element_type=F32)
             + jnp.dot(tri, mid, preferred_element_type=F32)
             + jnp.dot(tri, lo, preferred_element_type=F32)) + carry
        carry = c[t - 1:t, :]
        bias = jnp.where(rloc + blk * t >= padf, -LOG2E * c, MASK_VALUE)
        bh, bm, bl = _split3(bias)
        h8 = FOX_HEADS
        packed = jnp.where(
            lane < h8, bh.astype(F32),
            jnp.where(lane < 2 * h8, pltpu.roll(bm.astype(F32), h8, 1),
                      jnp.where(lane < 3 * h8, pltpu.roll(bl.astype(F32), 2 * h8, 1), 0.0)))
        e_ref[blk * t:(blk + 1) * t, :] = packed.astype(BF16)


def _gates(ff, fb, batch, lp, padf):
    return pl.pallas_call(
        functools.partial(_gates_kernel, padf=padf),
        grid=(batch,),
        in_specs=[pl.BlockSpec((lp, LANES), lambda b: (b, 0)),
                  pl.BlockSpec((1, LANES), lambda b: (0, 0))],
        out_specs=pl.BlockSpec((lp, LANES), lambda b: (b, 0)),
        out_shape=jax.ShapeDtypeStruct((batch * lp, LANES), BF16),
        compiler_params=pltpu.CompilerParams(
            dimension_semantics=("arbitrary",), vmem_limit_bytes=VMEM_LIMIT),
        name="fox_gates",
    )(ff, fb)


def _fox_kernel(qt_ref, k_ref, vt_ref, e_ref, o_ref):
    i = pl.program_id(1)
    t = SEQ_TILE
    half = FOX_HEAD_DIM
    npair = FOX_HEADS // 2
    n_one = 16
    srow = lax.broadcasted_iota(jnp.int32, (LANES, t), 0)
    key_i = lax.broadcasted_iota(jnp.int32, (t, 2 * t), 0)
    qry_i = lax.broadcasted_iota(jnp.int32, (t, 2 * t), 1)
    causal = key_i <= jnp.where(qry_i >= t, qry_i - t, qry_i)
    ones_rows = jnp.ones((n_one, t), BF16)
    lanes_of = lambda p: slice(p * LANES, (p + 1) * LANES)

    def stacked_qt(p):
        top = qt_ref[0, lanes_of(p), :]
        zero = jnp.zeros_like(top)
        sel_a = ((srow == 2 * p) | (srow == FOX_HEADS + 2 * p)
                 | (srow == 2 * FOX_HEADS + 2 * p)).astype(BF16)
        sel_b = ((srow == 2 * p + 1) | (srow == FOX_HEADS + 2 * p + 1)
                 | (srow == 2 * FOX_HEADS + 2 * p + 1)).astype(BF16)
        col_a = jnp.concatenate([jnp.where(srow < half, top, zero), sel_a], axis=0)
        col_b = jnp.concatenate([jnp.where(srow >= half, top, zero), sel_b], axis=0)
        return jnp.concatenate([col_a, col_b], axis=1)

    qqt = [stacked_qt(p) for p in range(npair)]

    def step(j, carry, masked):
        r0 = pl.multiple_of(j * t, t)
        ebias = e_ref[pl.ds(r0, t), :]
        scores = []
        for p in range(npair):
            kaug = jnp.concatenate([k_ref[pl.ds(r0, t), lanes_of(p)], ebias], axis=1)
            st = jnp.dot(kaug, qqt[p], preferred_element_type=F32)
            scores.append(jnp.where(causal, st, MASK_VALUE) if masked else st)
        probs = []
        for p in range(npair):
            m_old = carry[p][0]
            m_new = jnp.maximum(m_old, jnp.max(scores[p], axis=0, keepdims=True))
            probs.append((m_new, jnp.exp2(m_old - m_new),
                          jnp.exp2(scores[p] - m_new).astype(BF16)))
        out = []
        for p in range(npair):
            m_new, alpha, pr = probs[p]
            vaug = jnp.concatenate([vt_ref[j, lanes_of(p), :], ones_rows], axis=0)
            pv = jnp.dot(vaug, pr, preferred_element_type=F32)
            out.append((m_new, alpha * carry[p][1] + pv))
        return tuple(out)

    init = tuple((jnp.full((1, 2 * t), -jnp.inf, F32), jnp.zeros((LANES + n_one, 2 * t), F32))
                 for _ in range(npair))
    carry = lax.fori_loop(0, i, lambda j, c: step(j, c, False), init)
    fin = step(i, carry, True)
    for p in range(npair):
        acc = fin[p][1]
        ot = acc[0:LANES] / acc[LANES:LANES + 1]
        pair_t = jnp.concatenate([ot[0:half, 0:t], ot[half:LANES, t:2 * t]], axis=0)
        o_ref[:, lanes_of(p)] = pair_t.T.astype(o_ref.dtype)


def _fox(qt, k, vt, e, batch, lp):
    nq = lp // SEQ_TILE
    return pl.pallas_call(
        _fox_kernel,
        grid=(batch, nq),
        in_specs=[
            pl.BlockSpec((1, FOX_WIDTH, SEQ_TILE), lambda b, i: (b * nq + i, 0, 0)),
            pl.BlockSpec((lp, FOX_WIDTH), lambda b, i: (b, 0)),
            pl.BlockSpec((nq, FOX_WIDTH, SEQ_TILE), lambda b, i: (b, 0, 0)),
            pl.BlockSpec((lp, LANES), lambda b, i: (b, 0)),
        ],
        out_specs=pl.BlockSpec((SEQ_TILE, FOX_WIDTH), lambda b, i: (b * nq + i, 0)),
        out_shape=jax.ShapeDtypeStruct((batch * lp, FOX_WIDTH), BF16),
        compiler_params=pltpu.CompilerParams(
            dimension_semantics=("arbitrary", "arbitrary"), vmem_limit_bytes=VMEM_LIMIT),
        name="fox_attention",
    )(qt, k, vt, e)


def _hgrn_constants():
    c = CHUNK
    m = np.zeros((N_EXP_ROWS, c), np.float32)
    upper = np.zeros((N_LEVELS, c), np.float32)
    for t in range(c):
        m[t, :t + 1] = 1.0
        m[c + t, t + 1:] = 1.0
        for lv in range(N_LEVELS):
            w = c >> (lv + 1)
            mid = (t // (2 * w)) * 2 * w + w
            if t >= mid:
                m[(2 + lv) * c + t, mid:t + 1] = 1.0
                upper[lv, t] = 1.0
            else:
                m[(2 + lv) * c + t, t + 1:mid] = 1.0
    level = np.full((c, c), N_LEVELS + 1, np.int32)
    for t in range(c):
        level[t, t] = N_LEVELS
        for s in range(t):
            w = 1 << int(np.floor(np.log2(t ^ s)))
            level[t, s] = int(np.log2(c // (2 * w)))
    m3 = np.concatenate([m, m, m], axis=1)
    return m3, upper, level


def _hgrn_kernel(hq_ref, hf_ref, hi_ref, hg_ref, lbraw_ref, nw_ref, m3_ref, up_ref, lvl_ref,
                 o_ref, st_ref, *, layer):
    @pl.when(pl.program_id(1) == 0)
    def _():
        st_ref[...] = jnp.zeros_like(st_ref)

    c = CHUNK
    raw = lbraw_ref[...]
    ex = jnp.exp(raw - jnp.max(raw, axis=0, keepdims=True))
    s_lb = ex / jnp.sum(ex, axis=0, keepdims=True)
    lb = jnp.sum(s_lb[:layer + 1], axis=0, keepdims=True) - s_lb[0:1]
    level = lvl_ref[...]
    nt = (((1,), (1,)), ((), ()))
    tn = (((0,), (0,)), ((), ()))
    n_chunks = hq_ref.shape[0] // c
    heads = [slice(h * HG_K, (h + 1) * HG_K) for h in range(HG_HEADS)]

    one_m_lb = 1.0 - lb
    floor2 = LOG_F_MIN * LOG2E
    level_is = [level == lv for lv in range(N_LEVELS + 1)]
    gates, keys, queries = [], [], []
    for ci in range(n_chunks):
        z = hf_ref[ci * c:(ci + 1) * c, :]
        omf = one_m_lb / (1.0 + jnp.exp2(LOG2E * z))
        lg2 = jnp.log2(1.0 - omf)
        gates.append(jnp.maximum(lg2, floor2))
        keys.append(jnp.where(lg2 >= floor2, omf, 1.0 - np.exp(LOG_F_MIN)))
        queries.append(_silu(hq_ref[ci * c:(ci + 1) * c, :], HG_K ** -0.5))
    exps = [jnp.dot(m3_ref[...], jnp.concatenate(_split3(g), axis=0),
                    preferred_element_type=F32) for g in gates]

    o_intra, q_dec, k_dec, dec_last = [], [], [], []
    for ci in range(n_chunks):
        q, kk, e_all = queries[ci], keys[ci], exps[ci]
        v = hi_ref[ci * c:(ci + 1) * c, :]
        b = e_all[0:c]
        q_dec.append((q * jnp.exp2(b)).astype(BF16))
        k_dec.append((kk * jnp.exp2(e_all[c:2 * c])).astype(BF16))
        dec_last.append(jnp.exp2(b[c - 1:c, :]))
        rl = [(jnp.where(up_ref[lv] > 0.5, q, kk)
               * jnp.exp2(e_all[(2 + lv) * c:(3 + lv) * c])).astype(BF16)
              for lv in range(N_LEVELS)]
        qb, kb = q.astype(BF16), kk.astype(BF16)
        per_head = []
        for hs in heads:
            a = jnp.where(level_is[N_LEVELS],
                          lax.dot_general(qb[:, hs], kb[:, hs], nt, preferred_element_type=F32),
                          0.0)
            for lv in range(N_LEVELS):
                r = rl[lv][:, hs]
                a = jnp.where(level_is[lv],
                              lax.dot_general(r, r, nt, preferred_element_type=F32), a)
            per_head.append(jnp.dot(a.astype(BF16), v[:, hs], preferred_element_type=F32))
        o_intra.append(per_head)

    states = [st_ref[h] for h in range(HG_HEADS)]
    for ci in range(n_chunks):
        rows = slice(ci * c, (ci + 1) * c)
        v = hi_ref[rows, :]
        gate = _silu(hg_ref[rows, :])
        for h, hs in enumerate(heads):
            st = states[h]
            o = o_intra[ci][h] + lax.dot_general(q_dec[ci][:, hs], st.astype(BF16), nt,
                                                 preferred_element_type=F32)
            states[h] = st * dec_last[ci][:, hs] + lax.dot_general(
                v[:, hs], k_dec[ci][:, hs], tn, preferred_element_type=F32)
            o_ref[rows, hs] = (_rms(o, nw_ref[...]) * gate[:, hs]).astype(o_ref.dtype)
    for h in range(HG_HEADS):
        st_ref[h] = states[h]


def _hgrn(hq, hf, hi, hg, lb_raw, nw, layer, batch, lp):
    nb = lp // SEQ_TILE
    m3, upper, level = _hgrn_constants()
    row_spec = pl.BlockSpec((SEQ_TILE, HG_WIDTH), lambda b, i: (b * nb + i, 0))
    full = lambda shape: pl.BlockSpec(shape, lambda b, i: (0,) * len(shape))
    depth = lb_raw.shape[0]
    return pl.pallas_call(
        functools.partial(_hgrn_kernel, layer=layer),
        grid=(batch, nb),
        in_specs=[row_spec, row_spec, row_spec, row_spec,
                  full((depth, HG_WIDTH)), full((1, HG_V)),
                  full(m3.shape), full((N_LEVELS, CHUNK, 1)), full(level.shape)],
        out_specs=row_spec,
        out_shape=jax.ShapeDtypeStruct((batch * lp, HG_WIDTH), BF16),
        scratch_shapes=[pltpu.VMEM((HG_HEADS, HG_V, HG_K), F32)],
        compiler_params=pltpu.CompilerParams(
            dimension_semantics=("arbitrary", "arbitrary"), vmem_limit_bytes=VMEM_LIMIT),
        name="hgrn2",
    )(hq, hf, hi, hg, lb_raw, nw, jnp.asarray(m3, BF16),
      jnp.asarray(upper[:, :, None], F32), jnp.asarray(level))


def _mlp_kernel(h_ref, fo_ref, ho_ref, wo_ref, nw_ref, wg_ref, wu_ref, wd_ref, fnw_ref,
                out_ref, a_ref, *, final):
    mixed = (jnp.dot(fo_ref[...], wo_ref[0:FOX_WIDTH, :], preferred_element_type=F32)
             + jnp.dot(ho_ref[...], wo_ref[FOX_WIDTH:, :], preferred_element_type=F32))
    h1 = h_ref[...] + mixed
    u = _rms(h1, nw_ref[...]).astype(BF16)
    dff = wg_ref.shape[1]
    for c0 in range(0, dff, FF_TILE):
        cs = slice(c0, c0 + FF_TILE)
        gate = jnp.dot(u, wg_ref[:, cs], preferred_element_type=F32)
        up = jnp.dot(u, wu_ref[:, cs], preferred_element_type=F32)
        a_ref[:, cs] = (_silu(gate) * up).astype(BF16)
    h2 = h1 + jnp.dot(a_ref[...], wd_ref[...], preferred_element_type=F32)
    if final:
        h2 = _rms(h2, fnw_ref[...])
    out_ref[...] = h2


def _mlp(h, fo, ho, wo, nw, wg, wu, wd, fnw, final):
    rows, d = h.shape
    dff = wg.shape[1]
    row_spec = lambda c: pl.BlockSpec((ROW_TILE, c), lambda i: (i, 0))
    const = lambda shape: pl.BlockSpec(shape, lambda i: (0, 0), pipeline_mode=pl.Buffered(1))
    return pl.pallas_call(
        functools.partial(_mlp_kernel, final=final),
        grid=(rows // ROW_TILE,),
        in_specs=[row_spec(d), row_spec(FOX_WIDTH), row_spec(HG_WIDTH),
                  const(wo.shape), pl.BlockSpec((1, d), lambda i: (0, 0)),
                  const(wg.shape), const(wu.shape), const(wd.shape),
                  pl.BlockSpec((1, d), lambda i: (0, 0))],
        out_specs=row_spec(d),
        out_shape=jax.ShapeDtypeStruct((rows, d), F32),
        scratch_shapes=[pltpu.VMEM((ROW_TILE, dff), BF16)],
        compiler_params=pltpu.CompilerParams(
            dimension_semantics=("arbitrary",), vmem_limit_bytes=VMEM_LIMIT),
        name="mlp_final" if final else "mlp",
    )(h, fo, ho, wo, nw, wg, wu, wd, fnw)


def kernel(x, meta, norm_mix_w, w_in, fox_f_bias, hgrn_lb_raw, hgrn_norm_w, w_out,
           norm_ffn_w, w_ffn_gate, w_ffn_up, w_ffn_down, norm_final_w):
    batch, seq, d = x.shape
    depth = w_in.shape[0]
    real = N_META + seq
    lp = -(-real // SEQ_TILE) * SEQ_TILE
    while (batch * lp) % ROW_TILE:
        lp += SEQ_TILE
    padf = lp - real

    head = jnp.concatenate([jnp.zeros((padf, d), x.dtype), meta.astype(x.dtype)], axis=0)
    h = jnp.concatenate([jnp.broadcast_to(head[None], (batch, padf + N_META, d)), x], axis=1)
    h = h.reshape(batch * lp, d)

    fw = 3 * FOX_WIDTH
    lb_raw = hgrn_lb_raw.astype(F32)
    for l in range(depth):
        wl = w_in[l]
        w_cat = jnp.concatenate(
            [wl[:, :fw], wl[:, fw + FOX_HEADS:], wl[:, fw:fw + FOX_HEADS],
             jnp.zeros((d, LANES - FOX_HEADS), wl.dtype)], axis=1).astype(BF16)
        fb = jnp.concatenate([fox_f_bias[l].astype(F32),
                              jnp.zeros((LANES - FOX_HEADS,), F32)])[None]
        qt, k, vt, hq, hf, hi, hg, ff = _inproj(h, norm_mix_w[l][None].astype(F32), w_cat)
        e = _gates(ff, fb, batch, lp, padf)
        fox_o = _fox(qt, k, vt, e, batch, lp)
        hgrn_o = _hgrn(hq, hf, hi, hg, lb_raw, hgrn_norm_w[l][None].astype(F32), l, batch, lp)
        h = _mlp(h, fox_o, hgrn_o, w_out[l].astype(BF16), norm_ffn_w[l][None].astype(F32),
                 w_ffn_gate[l].astype(BF16), w_ffn_up[l].astype(BF16),
                 w_ffn_down[l].astype(BF16), norm_final_w[None].astype(F32),
                 final=(l == depth - 1))
    return h.reshape(batch, lp, d)[:, padf + N_META:]
```

```python
import functools

import numpy as np
import jax
import jax.numpy as jnp
from jax import lax
from jax.experimental import pallas as pl
from jax.experimental.pallas import tpu as pltpu

F32 = jnp.float32
BF16 = jnp.bfloat16

N_META = 16
FOX_HEADS = 8
FOX_HEAD_DIM = 64
FOX_WIDTH = FOX_HEADS * FOX_HEAD_DIM
HG_HEADS = 4
HG_K = 128
HG_V = 128
HG_WIDTH = HG_HEADS * HG_K
CHUNK = 64
EPS = 1e-6
MASK_VALUE = -1e30
LOG_F_MIN = -30.0
LOG2E = 1.4426950408889634

LANES = 128
SEQ_TILE = 256
ROW_TILE = 512
FF_TILE = 256
VMEM_LIMIT = 56 * 1024 * 1024

N_LEVELS = 6
N_EXP_ROWS = (2 + N_LEVELS) * CHUNK


def _rms(x, w):
    ms = jnp.mean(x * x, axis=-1, keepdims=True)
    return x * lax.rsqrt(ms + EPS) * w


def _split3(x):
    hi = x.astype(BF16)
    r1 = x - hi.astype(F32)
    mid = r1.astype(BF16)
    lo = (r1 - mid.astype(F32)).astype(BF16)
    return hi, mid, lo


def _silu(x, scale=1.0):
    return (scale * x) / (1.0 + jnp.exp2(-LOG2E * x))


def _inproj_kernel(h_ref, nw_ref, w_ref, qt_ref, k_ref, vt_ref, hq_ref, hf_ref, hi_ref,
                   hg_ref, ff_ref):
    u = _rms(h_ref[...], nw_ref[...]).astype(BF16)

    def mm(c0, c1):
        return jnp.dot(u, w_ref[:, c0:c1], preferred_element_type=F32)

    w = FOX_WIDTH
    q = mm(0, w) * (FOX_HEAD_DIM ** -0.5 * LOG2E)
    v = mm(2 * w, 3 * w)
    for s in range(ROW_TILE // SEQ_TILE):
        rows = slice(s * SEQ_TILE, (s + 1) * SEQ_TILE)
        qt_ref[s] = q[rows].T.astype(BF16)
        vt_ref[s] = v[rows].T.astype(BF16)
    k_ref[...] = mm(w, 2 * w).astype(BF16)
    o = 3 * w
    hq_ref[...] = mm(o, o + HG_WIDTH)
    hf_ref[...] = mm(o + HG_WIDTH, o + 2 * HG_WIDTH)
    hi_ref[...] = mm(o + 2 * HG_WIDTH, o + 3 * HG_WIDTH).astype(BF16)
    hg_ref[...] = mm(o + 3 * HG_WIDTH, o + 4 * HG_WIDTH)
    ff_ref[...] = mm(o + 4 * HG_WIDTH, o + 4 * HG_WIDTH + LANES)


def _inproj(h, nw, w):
    rows, d = h.shape
    ncol = w.shape[1]
    grid = (rows // ROW_TILE,)
    row_spec = lambda c: pl.BlockSpec((ROW_TILE, c), lambda i: (i, 0))
    slabs = ROW_TILE // SEQ_TILE
    slab_shape = jax.ShapeDtypeStruct((rows // SEQ_TILE, FOX_WIDTH, SEQ_TILE), BF16)
    slab_spec = pl.BlockSpec((slabs, FOX_WIDTH, SEQ_TILE), lambda i: (i, 0, 0))
    out_shape = (
        slab_shape,
        jax.ShapeDtypeStruct((rows, FOX_WIDTH), BF16),
        slab_shape,
        jax.ShapeDtypeStruct((rows, HG_WIDTH), F32),
        jax.ShapeDtypeStruct((rows, HG_WIDTH), F32),
        jax.ShapeDtypeStruct((rows, HG_WIDTH), BF16),
        jax.ShapeDtypeStruct((rows, HG_WIDTH), F32),
        jax.ShapeDtypeStruct((rows, LANES), F32),
    )
    return pl.pallas_call(
        _inproj_kernel,
        grid=grid,
        in_specs=[
            row_spec(d),
            pl.BlockSpec((1, d), lambda i: (0, 0)),
            pl.BlockSpec((d, ncol), lambda i: (0, 0), pipeline_mode=pl.Buffered(1)),
        ],
        out_specs=tuple(slab_spec if len(s.shape) == 3 else row_spec(s.shape[1])
                        for s in out_shape),
        out_shape=out_shape,
        compiler_params=pltpu.CompilerParams(
            dimension_semantics=("arbitrary",), vmem_limit_bytes=VMEM_LIMIT),
        name="inproj",
    )(h, nw, w)


def _gates_kernel(ff_ref, fb_ref, e_ref, *, padf):
    lp = ff_ref.shape[0]
    t = SEQ_TILE
    row = lax.broadcasted_iota(jnp.int32, (t, t), 0)
    col = lax.broadcasted_iota(jnp.int32, (t, t), 1)
    tri = (row >= col).astype(BF16)
    lane = lax.broadcasted_iota(jnp.int32, (t, LANES), 1)
    rloc = lax.broadcasted_iota(jnp.int32, (t, LANES), 0)
    carry = jnp.zeros((1, LANES), F32)
    for blk in range(lp // t):
        z = ff_ref[blk * t:(blk + 1) * t, :] + fb_ref[...]
        logf = jnp.minimum(z, 0.0) - jnp.log1p(jnp.exp(-jnp.abs(z)))
        hi, mid, lo = _split3(logf)
        c = (jnp.dot(tri, hi, preferred_element_type=F32)
             + jnp.dot(tri, mid, preferred_element_type=F32)
             + jnp.dot(tri, lo, preferred_element_type=F32)) + carry
        carry = c[t - 1:t, :]
        bias = jnp.where(rloc + blk * t >= padf, -LOG2E * c, MASK_VALUE)
        bh, bm, bl = _split3(bias)
        h8 = FOX_HEADS
        packed = jnp.where(
            lane < h8, bh.astype(F32),
            jnp.where(lane < 2 * h8, pltpu.roll(bm.astype(F32), h8, 1),
                      jnp.where(lane < 3 * h8, pltpu.roll(bl.astype(F32), 2 * h8, 1), 0.0)))
        e_ref[blk * t:(blk + 1) * t, :] = packed.astype(BF16)


def _gates(ff, fb, batch, lp, padf):
    return pl.pallas_call(
        functools.partial(_gates_kernel, padf=padf),
        grid=(batch,),
        in_specs=[pl.BlockSpec((lp, LANES), lambda b: (b, 0)),
                  pl.BlockSpec((1, LANES), lambda b: (0, 0))],
        out_specs=pl.BlockSpec((lp, LANES), lambda b: (b, 0)),
        out_shape=jax.ShapeDtypeStruct((batch * lp, LANES), BF16),
        compiler_params=pltpu.CompilerParams(
            dimension_semantics=("arbitrary",), vmem_limit_bytes=VMEM_LIMIT),
        name="fox_gates",
    )(ff, fb)


def _fox_kernel(qt_ref, k_ref, vt_ref, e_ref, o_ref):
    i = pl.program_id(1)
    t = SEQ_TILE
    half = FOX_HEAD_DIM
    npair = FOX_HEADS // 2
    n_one = 16
    srow = lax.broadcasted_iota(jnp.int32, (LANES, t), 0)
    key_i = lax.broadcasted_iota(jnp.int32, (t, 2 * t), 0)
    qry_i = lax.broadcasted_iota(jnp.int32, (t, 2 * t), 1)
    causal = key_i <= jnp.where(qry_i >= t, qry_i - t, qry_i)
    lanes_of = lambda p: slice(p * LANES, (p + 1) * LANES)

    def stacked_qt(p):
        top = qt_ref[0, lanes_of(p), :]
        zero = jnp.zeros_like(top)
        sel_a = ((srow == 2 * p) | (srow == FOX_HEADS + 2 * p)
                 | (srow == 2 * FOX_HEADS + 2 * p)).astype(BF16)
        sel_b = ((srow == 2 * p + 1) | (srow == FOX_HEADS + 2 * p + 1)
                 | (srow == 2 * FOX_HEADS + 2 * p + 1)).astype(BF16)
        col_a = jnp.concatenate([jnp.where(srow < half, top, zero), sel_a], axis=0)
        col_b = jnp.concatenate([jnp.where(srow >= half, top, zero), sel_b], axis=0)
        return jnp.concatenate([col_a, col_b], axis=1)

    qqt = [stacked_qt(p) for p in range(npair)]

    def step(j, nblk, carry, masked):
        rows = pl.ds(pl.multiple_of(j * t, t), nblk * t)
        ebias = e_ref[rows, :]
        scores = []
        for p in range(npair):
            kaug = jnp.concatenate([k_ref[rows, lanes_of(p)], ebias], axis=1)
            st = jnp.dot(kaug, qqt[p], preferred_element_type=F32)
            scores.append(jnp.where(causal, st, MASK_VALUE) if masked else st)
        probs = []
        for p in range(npair):
            m_old = carry[p][0]
            m_new = jnp.maximum(m_old, jnp.max(scores[p], axis=0, keepdims=True))
            probs.append((m_new, jnp.exp2(m_old - m_new),
                          jnp.exp2(scores[p] - m_new).astype(BF16)))
        out = []
        for p in range(npair):
            m_new, alpha, pr = probs[p]
            vt = jnp.concatenate([vt_ref[j + b, lanes_of(p), :] for b in range(nblk)], axis=1)
            vaug = jnp.concatenate([vt, jnp.ones((n_one, nblk * t), BF16)], axis=0)
            pv = jnp.dot(vaug, pr, preferred_element_type=F32)
            out.append((m_new, alpha * carry[p][1] + pv))
        return tuple(out)

    init = tuple((jnp.full((1, 2 * t), -jnp.inf, F32), jnp.zeros((LANES + n_one, 2 * t), F32))
                 for _ in range(npair))
    carry = lax.fori_loop(0, i // 2, lambda jj, c: step(2 * jj, 2, c, False), init)
    carry = lax.fori_loop(0, i % 2, lambda _, c: step(i - 1, 1, c, False), carry)
    fin = step(i, 1, carry, True)
    for p in range(npair):
        acc = fin[p][1]
        ot = acc[0:LANES] / acc[LANES:LANES + 1]
        pair_t = jnp.concatenate([ot[0:half, 0:t], ot[half:LANES, t:2 * t]], axis=0)
        o_ref[:, lanes_of(p)] = pair_t.T.astype(o_ref.dtype)


def _fox(qt, k, vt, e, batch, lp):
    nq = lp // SEQ_TILE
    return pl.pallas_call(
        _fox_kernel,
        grid=(batch, nq),
        in_specs=[
            pl.BlockSpec((1, FOX_WIDTH, SEQ_TILE), lambda b, i: (b * nq + i, 0, 0)),
            pl.BlockSpec((lp, FOX_WIDTH), lambda b, i: (b, 0)),
            pl.BlockSpec((nq, FOX_WIDTH, SEQ_TILE), lambda b, i: (b, 0, 0)),
            pl.BlockSpec((lp, LANES), lambda b, i: (b, 0)),
        ],
        out_specs=pl.BlockSpec((SEQ_TILE, FOX_WIDTH), lambda b, i: (b * nq + i, 0)),
        out_shape=jax.ShapeDtypeStruct((batch * lp, FOX_WIDTH), BF16),
        compiler_params=pltpu.CompilerParams(
            dimension_semantics=("arbitrary", "arbitrary"), vmem_limit_bytes=VMEM_LIMIT),
        name="fox_attention",
    )(qt, k, vt, e)


def _hgrn_constants():
    c = CHUNK
    m = np.zeros((N_EXP_ROWS, c), np.float32)
    upper = np.zeros((N_LEVELS, c), np.float32)
    for t in range(c):
        m[t, :t + 1] = 1.0
        m[c + t, t + 1:] = 1.0
        for lv in range(N_LEVELS):
            w = c >> (lv + 1)
            mid = (t // (2 * w)) * 2 * w + w
            if t >= mid:
                m[(2 + lv) * c + t, mid:t + 1] = 1.0
                upper[lv, t] = 1.0
            else:
                m[(2 + lv) * c + t, t + 1:mid] = 1.0
    level = np.full((c, c), N_LEVELS + 1, np.int32)
    for t in range(c):
        level[t, t] = N_LEVELS
        for s in range(t):
            w = 1 << int(np.floor(np.log2(t ^ s)))
            level[t, s] = int(np.log2(c // (2 * w)))
    m3 = np.concatenate([m, m, m], axis=1)
    return m3, upper, level


def _hgrn_kernel(hq_ref, hf_ref, hi_ref, hg_ref, lbraw_ref, nw_ref, m3_ref, up_ref, lvl_ref,
                 o_ref, st_ref, *, layer):
    @pl.when(pl.program_id(1) == 0)
    def _():
        st_ref[...] = jnp.zeros_like(st_ref)

    c = CHUNK
    raw = lbraw_ref[...]
    ex = jnp.exp(raw - jnp.max(raw, axis=0, keepdims=True))
    s_lb = ex / jnp.sum(ex, axis=0, keepdims=True)
    lb = jnp.sum(s_lb[:layer + 1], axis=0, keepdims=True) - s_lb[0:1]
    level = lvl_ref[...]
    nt = (((1,), (1,)), ((), ()))
    tn = (((0,), (0,)), ((), ()))
    n_chunks = hq_ref.shape[0] // c
    heads = [slice(h * HG_K, (h + 1) * HG_K) for h in range(HG_HEADS)]

    one_m_lb = 1.0 - lb
    floor2 = LOG_F_MIN * LOG2E
    level_is = [level == lv for lv in range(N_LEVELS + 1)]
    gates, keys, queries = [], [], []
    for ci in range(n_chunks):
        z = hf_ref[ci * c:(ci + 1) * c, :]
        omf = one_m_lb / (1.0 + jnp.exp2(LOG2E * z))
        lg2 = jnp.log2(1.0 - omf)
        gates.append(jnp.maximum(lg2, floor2))
        keys.append(jnp.where(lg2 >= floor2, omf, 1.0 - np.exp(LOG_F_MIN)))
        queries.append(_silu(hq_ref[ci * c:(ci + 1) * c, :], HG_K ** -0.5))
    exps = [jnp.dot(m3_ref[...], jnp.concatenate(_split3(g), axis=0),
                    preferred_element_type=F32) for g in gates]

    o_intra, q_dec, k_dec, dec_last = [], [], [], []
    for ci in range(n_chunks):
        q, kk, e_all = queries[ci], keys[ci], exps[ci]
        v = hi_ref[ci * c:(ci + 1) * c, :]
        b = e_all[0:c]
        q_dec.append((q * jnp.exp2(b)).astype(BF16))
        k_dec.append((kk * jnp.exp2(e_all[c:2 * c])).astype(BF16))
        dec_last.append(jnp.exp2(b[c - 1:c, :]))
        rl = [(jnp.where(up_ref[lv] > 0.5, q, kk)
               * jnp.exp2(e_all[(2 + lv) * c:(3 + lv) * c])).astype(BF16)
              for lv in range(N_LEVELS)]
        qb, kb = q.astype(BF16), kk.astype(BF16)
        per_head = []
        for hs in heads:
            a = jnp.where(level_is[N_LEVELS],
                          lax.dot_general(qb[:, hs], kb[:, hs], nt, preferred_element_type=F32),
                          0.0)
            for lv in range(N_LEVELS):
                r = rl[lv][:, hs]
                a = jnp.where(level_is[lv],
                              lax.dot_general(r, r, nt, preferred_element_type=F32), a)
            per_head.append(jnp.dot(a.astype(BF16), v[:, hs], preferred_element_type=F32))
        o_intra.append(per_head)

    states = [st_ref[h] for h in range(HG_HEADS)]
    for ci in range(n_chunks):
        rows = slice(ci * c, (ci + 1) * c)
        v = hi_ref[rows, :]
        gate = _silu(hg_ref[rows, :])
        for h, hs in enumerate(heads):
            st = states[h]
            o = o_intra[ci][h] + lax.dot_general(q_dec[ci][:, hs], st.astype(BF16), nt,
                                                 preferred_element_type=F32)
            states[h] = st * dec_last[ci][:, hs] + lax.dot_general(
                v[:, hs], k_dec[ci][:, hs], tn, preferred_element_type=F32)
            o_ref[rows, hs] = (_rms(o, nw_ref[...]) * gate[:, hs]).astype(o_ref.dtype)
    for h in range(HG_HEADS):
        st_ref[h] = states[h]


def _hgrn(hq, hf, hi, hg, lb_raw, nw, layer, batch, lp):
    nb = lp // SEQ_TILE
    m3, upper, level = _hgrn_constants()
    row_spec = pl.BlockSpec((SEQ_TILE, HG_WIDTH), lambda b, i: (b * nb + i, 0))
    full = lambda shape: pl.BlockSpec(shape, lambda b, i: (0,) * len(shape))
    depth = lb_raw.shape[0]
    return pl.pallas_call(
        functools.partial(_hgrn_kernel, layer=layer),
        grid=(batch, nb),
        in_specs=[row_spec, row_spec, row_spec, row_spec,
                  full((depth, HG_WIDTH)), full((1, HG_V)),
                  full(m3.shape), full((N_LEVELS, CHUNK, 1)), full(level.shape)],
        out_specs=row_spec,
        out_shape=jax.ShapeDtypeStruct((batch * lp, HG_WIDTH), BF16),
        scratch_shapes=[pltpu.VMEM((HG_HEADS, HG_V, HG_K), F32)],
        compiler_params=pltpu.CompilerParams(
            dimension_semantics=("arbitrary", "arbitrary"), vmem_limit_bytes=VMEM_LIMIT),
        name="hgrn2",
    )(hq, hf, hi, hg, lb_raw, nw, jnp.asarray(m3, BF16),
      jnp.asarray(upper[:, :, None], F32), jnp.asarray(level))


def _mlp_kernel(h_ref, fo_ref, ho_ref, wo_ref, nw_ref, wg_ref, wu_ref, wd_ref, fnw_ref,
                out_ref, a_ref, *, final):
    mixed = (jnp.dot(fo_ref[...], wo_ref[0:FOX_WIDTH, :], preferred_element_type=F32)
             + jnp.dot(ho_ref[...], wo_ref[FOX_WIDTH:, :], preferred_element_type=F32))
    h1 = h_ref[...] + mixed
    u = _rms(h1, nw_ref[...]).astype(BF16)
    dff = wg_ref.shape[1]
    for c0 in range(0, dff, FF_TILE):
        cs = slice(c0, c0 + FF_TILE)
        gate = jnp.dot(u, wg_ref[:, cs], preferred_element_type=F32)
        up = jnp.dot(u, wu_ref[:, cs], preferred_element_type=F32)
        a_ref[:, cs] = (_silu(gate) * up).astype(BF16)
    h2 = h1 + jnp.dot(a_ref[...], wd_ref[...], preferred_element_type=F32)
    if final:
        h2 = _rms(h2, fnw_ref[...])
    out_ref[...] = h2


def _mlp(h, fo, ho, wo, nw, wg, wu, wd, fnw, final):
    rows, d = h.shape
    dff = wg.shape[1]
    row_spec = lambda c: pl.BlockSpec((ROW_TILE, c), lambda i: (i, 0))
    const = lambda shape: pl.BlockSpec(shape, lambda i: (0, 0), pipeline_mode=pl.Buffered(1))
    return pl.pallas_call(
        functools.partial(_mlp_kernel, final=final),
        grid=(rows // ROW_TILE,),
        in_specs=[row_spec(d), row_spec(FOX_WIDTH), row_spec(HG_WIDTH),
                  const(wo.shape), pl.BlockSpec((1, d), lambda i: (0, 0)),
                  const(wg.shape), const(wu.shape), const(wd.shape),
                  pl.BlockSpec((1, d), lambda i: (0, 0))],
        out_specs=row_spec(d),
        out_shape=jax.ShapeDtypeStruct((rows, d), F32),
        scratch_shapes=[pltpu.VMEM((ROW_TILE, dff), BF16)],
        compiler_params=pltpu.CompilerParams(
            dimension_semantics=("arbitrary",), vmem_limit_bytes=VMEM_LIMIT),
        name="mlp_final" if final else "mlp",
    )(h, fo, ho, wo, nw, wg, wu, wd, fnw)


def kernel(x, meta, norm_mix_w, w_in, fox_f_bias, hgrn_lb_raw, hgrn_norm_w, w_out,
           norm_ffn_w, w_ffn_gate, w_ffn_up, w_ffn_down, norm_final_w):
    batch, seq, d = x.shape
    depth = w_in.shape[0]
    real = N_META + seq
    lp = -(-real // SEQ_TILE) * SEQ_TILE
    while (batch * lp) % ROW_TILE:
        lp += SEQ_TILE
    padf = lp - real

    head = jnp.concatenate([jnp.zeros((padf, d), x.dtype), meta.astype(x.dtype)], axis=0)
    h = jnp.concatenate([jnp.broadcast_to(head[None], (batch, padf + N_META, d)), x], axis=1)
    h = h.reshape(batch * lp, d)

    fw = 3 * FOX_WIDTH
    lb_raw = hgrn_lb_raw.astype(F32)
    for l in range(depth):
        wl = w_in[l]
        w_cat = jnp.concatenate(
            [wl[:, :fw], wl[:, fw + FOX_HEADS:], wl[:, fw:fw + FOX_HEADS],
             jnp.zeros((d, LANES - FOX_HEADS), wl.dtype)], axis=1).astype(BF16)
        fb = jnp.concatenate([fox_f_bias[l].astype(F32),
                              jnp.zeros((LANES - FOX_HEADS,), F32)])[None]
        qt, k, vt, hq, hf, hi, hg, ff = _inproj(h, norm_mix_w[l][None].astype(F32), w_cat)
        e = _gates(ff, fb, batch, lp, padf)
        fox_o = _fox(qt, k, vt, e, batch, lp)
        hgrn_o = _hgrn(hq, hf, hi, hg, lb_raw, hgrn_norm_w[l][None].astype(F32), l, batch, lp)
        h = _mlp(h, fox_o, hgrn_o, w_out[l].astype(BF16), norm_ffn_w[l][None].astype(F32),
                 w_ffn_gate[l].astype(BF16), w_ffn_up[l].astype(BF16),
                 w_ffn_down[l].astype(BF16), norm_final_w[None].astype(F32),
                 final=(l == depth - 1))
    return h.reshape(batch, lp, d)[:, padf + N_META:]
```

```python
import functools

import numpy as np
import jax
import jax.numpy as jnp
from jax import lax
from jax.experimental import pallas as pl
from jax.experimental.pallas import tpu as pltpu

F32 = jnp.float32
BF16 = jnp.bfloat16

N_META = 16
FOX_HEADS = 8
FOX_HEAD_DIM = 64
FOX_WIDTH = FOX_HEADS * FOX_HEAD_DIM
HG_HEADS = 4
HG_K = 128
HG_V = 128
HG_WIDTH = HG_HEADS * HG_K
CHUNK = 64
EPS = 1e-6
MASK_VALUE = -1e30
LOG_F_MIN = -30.0
LOG2E = 1.4426950408889634
SKIP_LOG2 = 160.0
NORM_SLACK = 1.02

LANES = 128
SEQ_TILE = 256
ROW_TILE = 512
FF_TILE = 256
VMEM_LIMIT = 56 * 1024 * 1024

N_LEVELS = 6
N_EXP_ROWS = (2 + N_LEVELS) * CHUNK


def _rms(x, w):
    ms = jnp.mean(x * x, axis=-1, keepdims=True)
    return x * lax.rsqrt(ms + EPS) * w


def _split3(x):
    hi = x.astype(BF16)
    r1 = x - hi.astype(F32)
    mid = r1.astype(BF16)
    lo = (r1 - mid.astype(F32)).astype(BF16)
    return hi, mid, lo


def _silu(x, scale=1.0):
    return (scale * x) / (1.0 + jnp.exp2(-LOG2E * x))


def _inproj_kernel(h_ref, nw_ref, w_ref, qt_ref, k_ref, vt_ref, hq_ref, hf_ref, hi_ref,
                   hg_ref, ff_ref):
    u = _rms(h_ref[...], nw_ref[...]).astype(BF16)

    def mm(c0, c1):
        return jnp.dot(u, w_ref[:, c0:c1], preferred_element_type=F32)

    w = FOX_WIDTH
    q = mm(0, w) * (FOX_HEAD_DIM ** -0.5 * LOG2E)
    v = mm(2 * w, 3 * w)
    for s in range(ROW_TILE // SEQ_TILE):
        rows = slice(s * SEQ_TILE, (s + 1) * SEQ_TILE)
        qt_ref[s] = q[rows].T.astype(BF16)
        vt_ref[s] = v[rows].T.astype(BF16)
    o = 3 * w
    hq_ref[...] = mm(o, o + HG_WIDTH)
    hf_ref[...] = mm(o + HG_WIDTH, o + 2 * HG_WIDTH)
    hi_ref[...] = mm(o + 2 * HG_WIDTH, o + 3 * HG_WIDTH).astype(BF16)
    hg_ref[...] = mm(o + 3 * HG_WIDTH, o + 4 * HG_WIDTH)
    kf = mm(w, 2 * w)
    k_ref[...] = kf.astype(BF16)
    col = lax.broadcasted_iota(jnp.int32, (w, LANES), 0)
    lane = lax.broadcasted_iota(jnp.int32, (w, LANES), 1)
    head = jnp.right_shift(col, int(np.log2(FOX_HEAD_DIM)))
    sel_q = (lane == FOX_HEADS + head).astype(BF16)
    sel_k = (lane == 2 * FOX_HEADS + head).astype(BF16)
    ff_ref[...] = (mm(o + 4 * HG_WIDTH, o + 4 * HG_WIDTH + LANES)
                   + jnp.dot((q * q).astype(BF16), sel_q, preferred_element_type=F32)
                   + jnp.dot((kf * kf).astype(BF16), sel_k, preferred_element_type=F32))


def _inproj(h, nw, w):
    rows, d = h.shape
    ncol = w.shape[1]
    grid = (rows // ROW_TILE,)
    row_spec = lambda c: pl.BlockSpec((ROW_TILE, c), lambda i: (i, 0))
    slabs = ROW_TILE // SEQ_TILE
    slab_shape = jax.ShapeDtypeStruct((rows // SEQ_TILE, FOX_WIDTH, SEQ_TILE), BF16)
    slab_spec = pl.BlockSpec((slabs, FOX_WIDTH, SEQ_TILE), lambda i: (i, 0, 0))
    out_shape = (
        slab_shape,
        jax.ShapeDtypeStruct((rows, FOX_WIDTH), BF16),
        slab_shape,
        jax.ShapeDtypeStruct((rows, HG_WIDTH), F32),
        jax.ShapeDtypeStruct((rows, HG_WIDTH), F32),
        jax.ShapeDtypeStruct((rows, HG_WIDTH), BF16),
        jax.ShapeDtypeStruct((rows, HG_WIDTH), F32),
        jax.ShapeDtypeStruct((rows, LANES), F32),
    )
    return pl.pallas_call(
        _inproj_kernel,
        grid=grid,
        in_specs=[
            row_spec(d),
            pl.BlockSpec((1, d), lambda i: (0, 0)),
            pl.BlockSpec((d, ncol), lambda i: (0, 0), pipeline_mode=pl.Buffered(1)),
        ],
        out_specs=tuple(slab_spec if len(s.shape) == 3 else row_spec(s.shape[1])
                        for s in out_shape),
        out_shape=out_shape,
        compiler_params=pltpu.CompilerParams(
            dimension_semantics=("arbitrary",), vmem_limit_bytes=VMEM_LIMIT),
        name="inproj",
    )(h, nw, w)


def _gates_kernel(ff_ref, fb_ref, e_ref, js_ref, *, padf):
    lp = ff_ref.shape[0]
    t = SEQ_TILE
    nblk = lp // t
    nrow = -(-nblk // 8) * 8
    h8 = FOX_HEADS
    row = lax.broadcasted_iota(jnp.int32, (t, t), 0)
    col = lax.broadcasted_iota(jnp.int32, (t, t), 1)
    tri = (row >= col).astype(BF16)
    lane = lax.broadcasted_iota(jnp.int32, (t, LANES), 1)
    rloc = lax.broadcasted_iota(jnp.int32, (t, LANES), 0)
    brow = lax.broadcasted_iota(jnp.int32, (nrow, LANES), 0)
    blane = lax.broadcasted_iota(jnp.int32, (nrow, LANES), 1)
    carry = jnp.zeros((1, LANES), F32)
    bmax = jnp.zeros((nrow, LANES), F32)
    bmin = jnp.zeros((nrow, LANES), F32)
    qnorm = jnp.zeros((nrow, LANES), F32)
    knorm = jnp.zeros((nrow, LANES), F32)
    for blk in range(nblk):
        raw = ff_ref[blk * t:(blk + 1) * t, :]
        z = raw + fb_ref[...]
        logf = jnp.minimum(z, 0.0) - jnp.log1p(jnp.exp(-jnp.abs(z)))
        hi, mid, lo = _split3(logf)
        c = (jnp.dot(tri, hi, preferred_element_type=F32)
             + jnp.dot(tri, mid, preferred_element_type=F32)
             + jnp.dot(tri, lo, preferred_element_type=F32)) + carry
        carry = c[t - 1:t, :]
        bias = jnp.where(rloc + blk * t >= padf, -LOG2E * c, MASK_VALUE)
        bh, bm, bl = _split3(bias)
        packed = jnp.where(
            lane < h8, bh.astype(F32),
            jnp.where(lane < 2 * h8, pltpu.roll(bm.astype(F32), h8, 1),
                      jnp.where(lane < 3 * h8, pltpu.roll(bl.astype(F32), 2 * h8, 1), 0.0)))
        e_ref[blk * t:(blk + 1) * t, :] = packed.astype(BF16)

        here = brow == blk
        sq = jnp.max(raw, axis=0, keepdims=True) * NORM_SLACK
        bmax = jnp.where(here, jnp.max(bias, axis=0, keepdims=True), bmax)
        bmin = jnp.where(here, jnp.min(bias, axis=0, keepdims=True), bmin)
        qnorm = jnp.where(here, jnp.sqrt(pltpu.roll(sq, LANES - h8, 1)), qnorm)
        knorm = jnp.where(here, jnp.sqrt(pltpu.roll(sq, LANES - 2 * h8, 1)), knorm)

    jlane = lax.broadcasted_iota(jnp.int32, js_ref.shape, 1)
    js = jnp.zeros(js_ref.shape, jnp.int32)
    rowf = brow[:, 0:1].astype(F32)
    for i in range(nblk):
        bound = qnorm[i:i + 1] * (knorm + knorm[i:i + 1]) + bmax - bmin[i:i + 1]
        needed = jnp.where((blane < h8) & (bound > -SKIP_LOG2), 1.0, 0.0)
        needed = jnp.max(needed, axis=1, keepdims=True)
        first = jnp.min(jnp.where((needed > 0.0) | (rowf >= i), rowf, float(nrow)),
                        axis=0, keepdims=True)
        js = jnp.where(jlane == i, first.astype(jnp.int32), js)
    js_ref[...] = js


def _gates(ff, fb, batch, lp, padf):
    return pl.pallas_call(
        functools.partial(_gates_kernel, padf=padf),
        grid=(batch,),
        in_specs=[pl.BlockSpec((lp, LANES), lambda b: (b, 0)),
                  pl.BlockSpec((1, LANES), lambda b: (0, 0))],
        out_specs=(pl.BlockSpec((lp, LANES), lambda b: (b, 0)),
                   pl.BlockSpec((8, LANES), lambda b: (b, 0))),
        out_shape=(jax.ShapeDtypeStruct((batch * lp, LANES), BF16),
                   jax.ShapeDtypeStruct((batch * 8, LANES), jnp.int32)),
        compiler_params=pltpu.CompilerParams(
            dimension_semantics=("arbitrary",), vmem_limit_bytes=VMEM_LIMIT),
        name="fox_gates",
    )(ff, fb)


def _fox_kernel(js_ref, qt_ref, k_ref, vt_ref, e_ref, o_ref):
    i = pl.program_id(1)
    t = SEQ_TILE
    half = FOX_HEAD_DIM
    npair = FOX_HEADS // 2
    n_one = 16
    srow = lax.broadcasted_iota(jnp.int32, (LANES, t), 0)
    key_i = lax.broadcasted_iota(jnp.int32, (t, 2 * t), 0)
    qry_i = lax.broadcasted_iota(jnp.int32, (t, 2 * t), 1)
    causal = key_i <= jnp.where(qry_i >= t, qry_i - t, qry_i)
    lanes_of = lambda p: slice(p * LANES, (p + 1) * LANES)

    def stacked_qt(p):
        top = qt_ref[0, lanes_of(p), :]
        zero = jnp.zeros_like(top)
        sel_a = ((srow == 2 * p) | (srow == FOX_HEADS + 2 * p)
                 | (srow == 2 * FOX_HEADS + 2 * p)).astype(BF16)
        sel_b = ((srow == 2 * p + 1) | (srow == FOX_HEADS + 2 * p + 1)
                 | (srow == 2 * FOX_HEADS + 2 * p + 1)).astype(BF16)
        col_a = jnp.concatenate([jnp.where(srow < half, top, zero), sel_a], axis=0)
        col_b = jnp.concatenate([jnp.where(srow >= half, top, zero), sel_b], axis=0)
        return jnp.concatenate([col_a, col_b], axis=1)

    qqt = [stacked_qt(p) for p in range(npair)]

    def scores_of(j, nblk, p, masked):
        rows = pl.ds(pl.multiple_of(j * t, t), nblk * t)
        kaug = jnp.concatenate([k_ref[rows, lanes_of(p)], e_ref[rows, :]], axis=1)
        st = jnp.dot(kaug, qqt[p], preferred_element_type=F32)
        return jnp.where(causal, st, MASK_VALUE) if masked else st

    def softmax_of(m_old, st):
        m_new = jnp.maximum(m_old, jnp.max(st, axis=0, keepdims=True))
        return m_new, jnp.exp2(m_old - m_new), jnp.exp2(st - m_new).astype(BF16)

    def pv_of(j, nblk, p, pr):
        vt = jnp.concatenate([vt_ref[j + b, lanes_of(p), :] for b in range(nblk)], axis=1)
        vaug = jnp.concatenate([vt, jnp.ones((n_one, nblk * t), BF16)], axis=0)
        return jnp.dot(vaug, pr, preferred_element_type=F32)

    def step(j, nblk, carry, masked):
        scores = [scores_of(j, nblk, p, masked) for p in range(npair)]
        probs = [softmax_of(carry[p][0], scores[p]) for p in range(npair)]
        return tuple((probs[p][0], probs[p][1] * carry[p][1] + pv_of(j, nblk, p, probs[p][2]))
                     for p in range(npair))

    init = tuple((jnp.full((1, 2 * t), -jnp.inf, F32), jnp.zeros((LANES + n_one, 2 * t), F32))
                 for _ in range(npair))
    j0 = js_ref[pl.program_id(0) * 8, i]
    n_vis = i - j0
    carry = lax.fori_loop(0, n_vis // 2, lambda jj, c: step(j0 + 2 * jj, 2, c, False), init)
    carry = lax.fori_loop(0, n_vis % 2, lambda _, c: step(i - 1, 1, c, False), carry)
    fin = step(i, 1, carry, True)
    for p in range(npair):
        acc = fin[p][1]
        ot = acc[0:LANES] / acc[LANES:LANES + 1]
        pair_t = jnp.concatenate([ot[0:half, 0:t], ot[half:LANES, t:2 * t]], axis=0)
        o_ref[:, lanes_of(p)] = pair_t.T.astype(o_ref.dtype)


def _fox(js, qt, k, vt, e, batch, lp):
    nq = lp // SEQ_TILE
    grid_spec = pltpu.PrefetchScalarGridSpec(
        num_scalar_prefetch=1,
        grid=(batch, nq),
        in_specs=[
            pl.BlockSpec((1, FOX_WIDTH, SEQ_TILE), lambda b, i, js: (b * nq + i, 0, 0)),
            pl.BlockSpec((lp, FOX_WIDTH), lambda b, i, js: (b, 0)),
            pl.BlockSpec((nq, FOX_WIDTH, SEQ_TILE), lambda b, i, js: (b, 0, 0)),
            pl.BlockSpec((lp, LANES), lambda b, i, js: (b, 0)),
        ],
        out_specs=pl.BlockSpec((SEQ_TILE, FOX_WIDTH), lambda b, i, js: (b * nq + i, 0)),
    )
    return pl.pallas_call(
        _fox_kernel,
        grid_spec=grid_spec,
        out_shape=jax.ShapeDtypeStruct((batch * lp, FOX_WIDTH), BF16),
        compiler_params=pltpu.CompilerParams(
            dimension_semantics=("arbitrary", "arbitrary"), vmem_limit_bytes=VMEM_LIMIT),
        name="fox_attention",
    )(js, qt, k, vt, e)


def _hgrn_constants():
    c = CHUNK
    m = np.zeros((N_EXP_ROWS, c), np.float32)
    upper = np.zeros((N_LEVELS, c), np.float32)
    for t in range(c):
        m[t, :t + 1] = 1.0
        m[c + t, t + 1:] = 1.0
        for lv in range(N_LEVELS):
            w = c >> (lv + 1)
            mid = (t // (2 * w)) * 2 * w + w
            if t >= mid:
                m[(2 + lv) * c + t, mid:t + 1] = 1.0
                upper[lv, t] = 1.0
            else:
                m[(2 + lv) * c + t, t + 1:mid] = 1.0
    level = np.full((c, c), N_LEVELS + 1, np.int32)
    for t in range(c):
        level[t, t] = N_LEVELS
        for s in range(t):
            w = 1 << int(np.floor(np.log2(t ^ s)))
            level[t, s] = int(np.log2(c // (2 * w)))
    m3 = np.concatenate([m, m, m], axis=1)
    return m3, upper, level


def _hgrn_kernel(hq_ref, hf_ref, hi_ref, hg_ref, lbraw_ref, nw_ref, m3_ref, up_ref, lvl_ref,
                 o_ref, st_ref, *, layer):
    @pl.when(pl.program_id(1) == 0)
    def _():
        st_ref[...] = jnp.zeros_like(st_ref)

    c = CHUNK
    raw = lbraw_ref[...]
    ex = jnp.exp(raw - jnp.max(raw, axis=0, keepdims=True))
    s_lb = ex / jnp.sum(ex, axis=0, keepdims=True)
    lb = jnp.sum(s_lb[:layer + 1], axis=0, keepdims=True) - s_lb[0:1]
    level = lvl_ref[...]
    nt = (((1,), (1,)), ((), ()))
    tn = (((0,), (0,)), ((), ()))
    n_chunks = hq_ref.shape[0] // c
    heads = [slice(h * HG_K, (h + 1) * HG_K) for h in range(HG_HEADS)]

    one_m_lb = 1.0 - lb
    floor2 = LOG_F_MIN * LOG2E
    level_is = [level == lv for lv in range(N_LEVELS + 1)]
    gates, keys, queries = [], [], []
    for ci in range(n_chunks):
        z = hf_ref[ci * c:(ci + 1) * c, :]
        omf = one_m_lb / (1.0 + jnp.exp2(LOG2E * z))
        lg2 = jnp.log2(1.0 - omf)
        gates.append(jnp.maximum(lg2, floor2))
        keys.append(jnp.where(lg2 >= floor2, omf, 1.0 - np.exp(LOG_F_MIN)))
        queries.append(_silu(hq_ref[ci * c:(ci + 1) * c, :], HG_K ** -0.5))
    exps = [jnp.dot(m3_ref[...], jnp.concatenate(_split3(g), axis=0),
                    preferred_element_type=F32) for g in gates]

    o_intra, q_dec, k_dec, dec_last = [], [], [], []
    for ci in range(n_chunks):
        q, kk, e_all = queries[ci], keys[ci], exps[ci]
        v = hi_ref[ci * c:(ci + 1) * c, :]
        b = e_all[0:c]
        q_dec.append((q * jnp.exp2(b)).astype(BF16))
        k_dec.append((kk * jnp.exp2(e_all[c:2 * c])).astype(BF16))
        dec_last.append(jnp.exp2(b[c - 1:c, :]))
        rl = [(jnp.where(up_ref[lv] > 0.5, q, kk)
               * jnp.exp2(e_all[(2 + lv) * c:(3 + lv) * c])).astype(BF16)
              for lv in range(N_LEVELS)]
        qb, kb = q.astype(BF16), kk.astype(BF16)
        per_head = []
        for hs in heads:
            a = jnp.where(level_is[N_LEVELS],
                          lax.dot_general(qb[:, hs], kb[:, hs], nt, preferred_element_type=F32),
                          0.0)
            for lv in range(N_LEVELS):
                r = rl[lv][:, hs]
                a = jnp.where(level_is[lv],
                              lax.dot_general(r, r, nt, preferred_element_type=F32), a)
            per_head.append(jnp.dot(a.astype(BF16), v[:, hs], preferred_element_type=F32))
        o_intra.append(per_head)

    states = [st_ref[h] for h in range(HG_HEADS)]
    for ci in range(n_chunks):
        rows = slice(ci * c, (ci + 1) * c)
        v = hi_ref[rows, :]
        gate = _silu(hg_ref[rows, :])
        for h, hs in enumerate(heads):
            st = states[h]
            o = o_intra[ci][h] + lax.dot_general(q_dec[ci][:, hs], st.astype(BF16), nt,
                                                 preferred_element_type=F32)
            states[h] = st * dec_last[ci][:, hs] + lax.dot_general(
                v[:, hs], k_dec[ci][:, hs], tn, preferred_element_type=F32)
            o_ref[rows, hs] = (_rms(o, nw_ref[...]) * gate[:, hs]).astype(o_ref.dtype)
    for h in range(HG_HEADS):
        st_ref[h] = states[h]


def _hgrn(hq, hf, hi, hg, lb_raw, nw, layer, batch, lp):
    nb = lp // SEQ_TILE
    m3, upper, level = _hgrn_constants()
    row_spec = pl.BlockSpec((SEQ_TILE, HG_WIDTH), lambda b, i: (b * nb + i, 0))
    full = lambda shape: pl.BlockSpec(shape, lambda b, i: (0,) * len(shape))
    depth = lb_raw.shape[0]
    return pl.pallas_call(
        functools.partial(_hgrn_kernel, layer=layer),
        grid=(batch, nb),
        in_specs=[row_spec, row_spec, row_spec, row_spec,
                  full((depth, HG_WIDTH)), full((1, HG_V)),
                  full(m3.shape), full((N_LEVELS, CHUNK, 1)), full(level.shape)],
        out_specs=row_spec,
        out_shape=jax.ShapeDtypeStruct((batch * lp, HG_WIDTH), BF16),
        scratch_shapes=[pltpu.VMEM((HG_HEADS, HG_V, HG_K), F32)],
        compiler_params=pltpu.CompilerParams(
            dimension_semantics=("arbitrary", "arbitrary"), vmem_limit_bytes=VMEM_LIMIT),
        name="hgrn2",
    )(hq, hf, hi, hg, lb_raw, nw, jnp.asarray(m3, BF16),
      jnp.asarray(upper[:, :, None], F32), jnp.asarray(level))


def _mlp_kernel(h_ref, fo_ref, ho_ref, wo_ref, nw_ref, wg_ref, wu_ref, wd_ref, fnw_ref,
                out_ref, a_ref, *, final):
    mixed = (jnp.dot(fo_ref[...], wo_ref[0:FOX_WIDTH, :], preferred_element_type=F32)
             + jnp.dot(ho_ref[...], wo_ref[FOX_WIDTH:, :], preferred_element_type=F32))
    h1 = h_ref[...] + mixed
    u = _rms(h1, nw_ref[...]).astype(BF16)
    dff = wg_ref.shape[1]
    for c0 in range(0, dff, FF_TILE):
        cs = slice(c0, c0 + FF_TILE)
        gate = jnp.dot(u, wg_ref[:, cs], preferred_element_type=F32)
        up = jnp.dot(u, wu_ref[:, cs], preferred_element_type=F32)
        a_ref[:, cs] = (_silu(gate) * up).astype(BF16)
    h2 = h1 + jnp.dot(a_ref[...], wd_ref[...], preferred_element_type=F32)
    if final:
        h2 = _rms(h2, fnw_ref[...])
    out_ref[...] = h2


def _mlp(h, fo, ho, wo, nw, wg, wu, wd, fnw, final):
    rows, d = h.shape
    dff = wg.shape[1]
    row_spec = lambda c: pl.BlockSpec((ROW_TILE, c), lambda i: (i, 0))
    const = lambda shape: pl.BlockSpec(shape, lambda i: (0, 0), pipeline_mode=pl.Buffered(1))
    return pl.pallas_call(
        functools.partial(_mlp_kernel, final=final),
        grid=(rows // ROW_TILE,),
        in_specs=[row_spec(d), row_spec(FOX_WIDTH), row_spec(HG_WIDTH),
                  const(wo.shape), pl.BlockSpec((1, d), lambda i: (0, 0)),
                  const(wg.shape), const(wu.shape), const(wd.shape),
                  pl.BlockSpec((1, d), lambda i: (0, 0))],
        out_specs=row_spec(d),
        out_shape=jax.ShapeDtypeStruct((rows, d), F32),
        scratch_shapes=[pltpu.VMEM((ROW_TILE, dff), BF16)],
        compiler_params=pltpu.CompilerParams(
            dimension_semantics=("arbitrary",), vmem_limit_bytes=VMEM_LIMIT),
        name="mlp_final" if final else "mlp",
    )(h, fo, ho, wo, nw, wg, wu, wd, fnw)


def kernel(x, meta, norm_mix_w, w_in, fox_f_bias, hgrn_lb_raw, hgrn_norm_w, w_out,
           norm_ffn_w, w_ffn_gate, w_ffn_up, w_ffn_down, norm_final_w):
    batch, seq, d = x.shape
    depth = w_in.shape[0]
    real = N_META + seq
    lp = -(-real // SEQ_TILE) * SEQ_TILE
    while (batch * lp) % ROW_TILE:
        lp += SEQ_TILE
    padf = lp - real

    head = jnp.concatenate([jnp.zeros((padf, d), x.dtype), meta.astype(x.dtype)], axis=0)
    h = jnp.concatenate([jnp.broadcast_to(head[None], (batch, padf + N_META, d)), x], axis=1)
    h = h.reshape(batch * lp, d)

    fw = 3 * FOX_WIDTH
    lb_raw = hgrn_lb_raw.astype(F32)
    for l in range(depth):
        wl = w_in[l]
        w_cat = jnp.concatenate(
            [wl[:, :fw], wl[:, fw + FOX_HEADS:], wl[:, fw:fw + FOX_HEADS],
             jnp.zeros((d, LANES - FOX_HEADS), wl.dtype)], axis=1).astype(BF16)
        fb = jnp.concatenate([fox_f_bias[l].astype(F32),
                              jnp.zeros((LANES - FOX_HEADS,), F32)])[None]
        qt, k, vt, hq, hf, hi, hg, ff = _inproj(h, norm_mix_w[l][None].astype(F32), w_cat)
        e, js = _gates(ff, fb, batch, lp, padf)
        fox_o = _fox(js, qt, k, vt, e, batch, lp)
        hgrn_o = _hgrn(hq, hf, hi, hg, lb_raw, hgrn_norm_w[l][None].astype(F32), l, batch, lp)
        h = _mlp(h, fox_o, hgrn_o, w_out[l].astype(BF16), norm_ffn_w[l][None].astype(F32),
                 w_ffn_gate[l].astype(BF16), w_ffn_up[l].astype(BF16),
                 w_ffn_down[l].astype(BF16), norm_final_w[None].astype(F32),
                 final=(l == depth - 1))
    return h.reshape(batch, lp, d)[:, padf + N_META:]
```

```python
import functools
from typing import NamedTuple

import numpy as np
import jax
import jax.numpy as jnp
from jax import lax
from jax.experimental import pallas as pl
from jax.experimental.pallas import tpu as pltpu

F32 = jnp.float32
BF16 = jnp.bfloat16

N_META = 16
FOX_HEADS = 8
FOX_HEAD_DIM = 64
FOX_WIDTH = FOX_HEADS * FOX_HEAD_DIM
HG_HEADS = 4
HG_K = 128
HG_V = 128
HG_WIDTH = HG_HEADS * HG_K
CHUNK = 64
EPS = 1e-6
MASK_VALUE = -1e30
LOG_F_MIN = -30.0
LOG2E = 1.4426950408889634
SKIP_LOG2 = 160.0
NORM_SLACK = 1.02

LANES = 128
SEQ_TILE = 256
ROW_TILE = 512
FF_TILE = 256
VMEM_LIMIT = 56 * 1024 * 1024

N_LEVELS = 6
N_EXP_ROWS = (2 + N_LEVELS) * CHUNK


def _rms(x, w):
    ms = jnp.mean(x * x, axis=-1, keepdims=True)
    return x * lax.rsqrt(ms + EPS) * w


def _split3(x):
    hi = x.astype(BF16)
    r1 = x - hi.astype(F32)
    mid = r1.astype(BF16)
    lo = (r1 - mid.astype(F32)).astype(BF16)
    return hi, mid, lo


def _silu(x, scale=1.0):
    return (scale * x) / (1.0 + jnp.exp2(-LOG2E * x))


CAST_ROWS = 256


def _cast_kernel(w_ref, o_ref):
    o_ref[...] = w_ref[...].astype(BF16)


def _to_bf16(w):
    depth, kdim, n = w.shape
    spec = pl.BlockSpec((1, CAST_ROWS, n), lambda l, r: (l, r, 0))
    return pl.pallas_call(
        _cast_kernel,
        grid=(depth, kdim // CAST_ROWS),
        in_specs=[spec],
        out_specs=spec,
        out_shape=jax.ShapeDtypeStruct(w.shape, BF16),
        compiler_params=pltpu.CompilerParams(
            dimension_semantics=("arbitrary", "arbitrary"), vmem_limit_bytes=VMEM_LIMIT),
        name="cast_weights",
    )(w)


def _split_w_in_kernel(w_ref, fox_ref, hg_ref, ff_ref):
    fw = 3 * FOX_WIDTH
    fox_ref[...] = w_ref[:, :, 0:fw].astype(BF16)
    hg_ref[...] = w_ref[:, :, fw + FOX_HEADS:].astype(BF16)
    blk = w_ref[:, :, fw:fw + LANES]
    lane = lax.broadcasted_iota(jnp.int32, blk.shape, 2)
    ff_ref[...] = jnp.where(lane < FOX_HEADS, blk, 0.0).astype(BF16)


def _split_w_in(w):
    depth, kdim, n = w.shape
    fw = 3 * FOX_WIDTH
    widths = (fw, n - fw - FOX_HEADS, LANES)
    spec = lambda c: pl.BlockSpec((1, CAST_ROWS, c), lambda l, r: (l, r, 0))
    return pl.pallas_call(
        _split_w_in_kernel,
        grid=(depth, kdim // CAST_ROWS),
        in_specs=[spec(n)],
        out_specs=tuple(spec(c) for c in widths),
        out_shape=tuple(jax.ShapeDtypeStruct((depth, kdim, c), BF16) for c in widths),
        compiler_params=pltpu.CompilerParams(
            dimension_semantics=("arbitrary", "arbitrary"), vmem_limit_bytes=VMEM_LIMIT),
        name="split_w_in",
    )(w)


def _layer_spec(a, layer, resident=False):
    tail = a.shape[1:]
    index = lambda *_: (layer,) + (0,) * len(tail)
    if resident:
        return pl.BlockSpec((None,) + tail, index, pipeline_mode=pl.Buffered(1))
    return pl.BlockSpec((None,) + tail, index)


class _Geom(NamedTuple):
    batch: int
    nb: int
    sb: int
    pad: int


def _stream_specs(first, g, d, block_of):
    def spec(half):
        def index(*idx):
            hh = block_of(*idx) + half
            if first:
                hh = (hh // g.nb) * g.sb + jnp.maximum(hh % g.nb - 1, 0)
            return (hh, 0, 0)
        return pl.BlockSpec((1, SEQ_TILE, d), index)
    return [spec(0), spec(1)]


def _stream_rows(head_ref, lo_ref, hi_ref, hh, first, g):
    lo, hi = lo_ref[0], hi_ref[0]
    if first:
        lo = jnp.where(hh % g.nb == 0, head_ref[...], lo)
        hi = jnp.where((hh + 1) % g.nb == 0, head_ref[...], hi)
    return jnp.concatenate([lo, hi], axis=0)


def _inproj_kernel(head_ref, lo_ref, hi_ref, nw_ref, wfox_ref, whg_ref, wff_ref, qt_ref, k_ref,
                   vt_ref, hq_ref, hf_ref, hi_out_ref, hg_ref, ff_ref, *, first, g):
    h = _stream_rows(head_ref, lo_ref, hi_ref, 2 * pl.program_id(0), first, g)
    u = _rms(h, nw_ref[...]).astype(BF16)

    def mm(c0, c1, w_ref=wfox_ref):
        return jnp.dot(u, w_ref[:, c0:c1], preferred_element_type=F32)

    w = FOX_WIDTH
    q = mm(0, w) * (FOX_HEAD_DIM ** -0.5 * LOG2E)
    v = mm(2 * w, 3 * w)
    for s in range(ROW_TILE // SEQ_TILE):
        rows = slice(s * SEQ_TILE, (s + 1) * SEQ_TILE)
        qt_ref[s] = q[rows].T.astype(BF16)
        vt_ref[s] = v[rows].T.astype(BF16)
    hq_ref[...] = mm(0, HG_WIDTH, whg_ref)
    hf_ref[...] = mm(HG_WIDTH, 2 * HG_WIDTH, whg_ref)
    hi_out_ref[...] = mm(2 * HG_WIDTH, 3 * HG_WIDTH, whg_ref).astype(BF16)
    hg_ref[...] = mm(3 * HG_WIDTH, 4 * HG_WIDTH, whg_ref)
    kf = mm(w, 2 * w)
    k_ref[...] = kf.astype(BF16)
    col = lax.broadcasted_iota(jnp.int32, (w, LANES), 0)
    lane = lax.broadcasted_iota(jnp.int32, (w, LANES), 1)
    head = jnp.right_shift(col, int(np.log2(FOX_HEAD_DIM)))
    sel_q = (lane == FOX_HEADS + head).astype(BF16)
    sel_k = (lane == 2 * FOX_HEADS + head).astype(BF16)
    ff_ref[...] = (mm(0, LANES, wff_ref)
                   + jnp.dot((q * q).astype(BF16), sel_q, preferred_element_type=F32)
                   + jnp.dot((kf * kf).astype(BF16), sel_k, preferred_element_type=F32))


def _inproj(head, stream, first, g, layer, nw, wfox, whg, wff):
    d = head.shape[1]
    rows = g.batch * g.nb * SEQ_TILE
    grid = (rows // ROW_TILE,)
    row_spec = lambda c: pl.BlockSpec((ROW_TILE, c), lambda i: (i, 0))
    slabs = ROW_TILE // SEQ_TILE
    slab_shape = jax.ShapeDtypeStruct((rows // SEQ_TILE, FOX_WIDTH, SEQ_TILE), BF16)
    slab_spec = pl.BlockSpec((slabs, FOX_WIDTH, SEQ_TILE), lambda i: (i, 0, 0))
    out_shape = (
        slab_shape,
        jax.ShapeDtypeStruct((rows, FOX_WIDTH), BF16),
        slab_shape,
        jax.ShapeDtypeStruct((rows, HG_WIDTH), F32),
        jax.ShapeDtypeStruct((rows, HG_WIDTH), F32),
        jax.ShapeDtypeStruct((rows, HG_WIDTH), BF16),
        jax.ShapeDtypeStruct((rows, HG_WIDTH), F32),
        jax.ShapeDtypeStruct((rows, LANES), F32),
    )
    return pl.pallas_call(
        functools.partial(_inproj_kernel, first=first, g=g),
        grid=grid,
        in_specs=[
            pl.BlockSpec(head.shape, lambda i: (0, 0)),
            *_stream_specs(first, g, d, lambda i: 2 * i),
            _layer_spec(nw, layer),
            _layer_spec(wfox, layer, True), _layer_spec(whg, layer, True),
            _layer_spec(wff, layer, True),
        ],
        out_specs=tuple(slab_spec if len(s.shape) == 3 else row_spec(s.shape[1])
                        for s in out_shape),
        out_shape=out_shape,
        compiler_params=pltpu.CompilerParams(
            dimension_semantics=("arbitrary",), vmem_limit_bytes=VMEM_LIMIT),
        name="inproj",
    )(head, stream, stream, nw, wfox, whg, wff)


def _gates_kernel(ff_ref, fb_ref, e_ref, js_ref, *, padf):
    lp = ff_ref.shape[0]
    t = SEQ_TILE
    nblk = lp // t
    nrow = -(-nblk // 8) * 8
    h8 = FOX_HEADS
    row = lax.broadcasted_iota(jnp.int32, (t, t), 0)
    col = lax.broadcasted_iota(jnp.int32, (t, t), 1)
    tri = (row >= col).astype(BF16)
    lane = lax.broadcasted_iota(jnp.int32, (t, LANES), 1)
    rloc = lax.broadcasted_iota(jnp.int32, (t, LANES), 0)
    brow = lax.broadcasted_iota(jnp.int32, (nrow, LANES), 0)
    blane = lax.broadcasted_iota(jnp.int32, (nrow, LANES), 1)
    carry = jnp.zeros((1, LANES), F32)
    bmax = jnp.zeros((nrow, LANES), F32)
    bmin = jnp.zeros((nrow, LANES), F32)
    qnorm = jnp.zeros((nrow, LANES), F32)
    knorm = jnp.zeros((nrow, LANES), F32)
    for blk in range(nblk):
        raw = ff_ref[blk * t:(blk + 1) * t, :]
        z = raw + fb_ref[...]
        logf = jnp.minimum(z, 0.0) - jnp.log1p(jnp.exp(-jnp.abs(z)))
        hi, mid, lo = _split3(logf)
        c = (jnp.dot(tri, hi, preferred_element_type=F32)
             + jnp.dot(tri, mid, preferred_element_type=F32)
             + jnp.dot(tri, lo, preferred_element_type=F32)) + carry
        carry = c[t - 1:t, :]
        bias = jnp.where(rloc + blk * t >= padf, -LOG2E * c, MASK_VALUE)
        bh, bm, bl = _split3(bias)
        packed = jnp.where(
            lane < h8, bh.astype(F32),
            jnp.where(lane < 2 * h8, pltpu.roll(bm.astype(F32), h8, 1),
                      jnp.where(lane < 3 * h8, pltpu.roll(bl.astype(F32), 2 * h8, 1), 0.0)))
        e_ref[blk * t:(blk + 1) * t, :] = packed.astype(BF16)

        here = brow == blk
        sq = jnp.max(raw, axis=0, keepdims=True) * NORM_SLACK
        bmax = jnp.where(here, jnp.max(bias, axis=0, keepdims=True), bmax)
        bmin = jnp.where(here, jnp.min(bias, axis=0, keepdims=True), bmin)
        qnorm = jnp.where(here, jnp.sqrt(pltpu.roll(sq, LANES - h8, 1)), qnorm)
        knorm = jnp.where(here, jnp.sqrt(pltpu.roll(sq, LANES - 2 * h8, 1)), knorm)

    jlane = lax.broadcasted_iota(jnp.int32, js_ref.shape, 1)
    js = jnp.zeros(js_ref.shape, jnp.int32)
    rowf = brow[:, 0:1].astype(F32)
    for i in range(nblk):
        bound = qnorm[i:i + 1] * (knorm + knorm[i:i + 1]) + bmax - bmin[i:i + 1]
        needed = jnp.where((blane < h8) & (bound > -SKIP_LOG2), 1.0, 0.0)
        needed = jnp.max(needed, axis=1, keepdims=True)
        first = jnp.min(jnp.where((needed > 0.0) | (rowf >= i), rowf, float(nrow)),
                        axis=0, keepdims=True)
        js = jnp.where(jlane == i, first.astype(jnp.int32), js)
    js_ref[...] = js


def _gates(ff, fb, layer, batch, lp, padf):
    return pl.pallas_call(
        functools.partial(_gates_kernel, padf=padf),
        grid=(batch,),
        in_specs=[pl.BlockSpec((lp, LANES), lambda b: (b, 0)),
                  _layer_spec(fb, layer)],
        out_specs=(pl.BlockSpec((lp, LANES), lambda b: (b, 0)),
                   pl.BlockSpec((8, LANES), lambda b: (b, 0))),
        out_shape=(jax.ShapeDtypeStruct((batch * lp, LANES), BF16),
                   jax.ShapeDtypeStruct((batch * 8, LANES), jnp.int32)),
        compiler_params=pltpu.CompilerParams(
            dimension_semantics=("arbitrary",), vmem_limit_bytes=VMEM_LIMIT),
        name="fox_gates",
    )(ff, fb)


def _fox_kernel(js_ref, qt_ref, k_ref, vt_ref, e_ref, o_ref):
    i = pl.program_id(1)
    t = SEQ_TILE
    half = FOX_HEAD_DIM
    npair = FOX_HEADS // 2
    n_one = 16
    srow = lax.broadcasted_iota(jnp.int32, (LANES, t), 0)
    key_i = lax.broadcasted_iota(jnp.int32, (t, 2 * t), 0)
    qry_i = lax.broadcasted_iota(jnp.int32, (t, 2 * t), 1)
    causal = key_i <= jnp.where(qry_i >= t, qry_i - t, qry_i)
    lanes_of = lambda p: slice(p * LANES, (p + 1) * LANES)

    def stacked_qt(p):
        top = qt_ref[0, lanes_of(p), :]
        zero = jnp.zeros_like(top)
        sel_a = ((srow == 2 * p) | (srow == FOX_HEADS + 2 * p)
                 | (srow == 2 * FOX_HEADS + 2 * p)).astype(BF16)
        sel_b = ((srow == 2 * p + 1) | (srow == FOX_HEADS + 2 * p + 1)
                 | (srow == 2 * FOX_HEADS + 2 * p + 1)).astype(BF16)
        col_a = jnp.concatenate([jnp.where(srow < half, top, zero), sel_a], axis=0)
        col_b = jnp.concatenate([jnp.where(srow >= half, top, zero), sel_b], axis=0)
        return jnp.concatenate([col_a, col_b], axis=1)

    qqt = [stacked_qt(p) for p in range(npair)]

    def scores_of(j, nblk, p, masked):
        rows = pl.ds(pl.multiple_of(j * t, t), nblk * t)
        kaug = jnp.concatenate([k_ref[rows, lanes_of(p)], e_ref[rows, :]], axis=1)
        st = jnp.dot(kaug, qqt[p], preferred_element_type=F32)
        return jnp.where(causal, st, MASK_VALUE) if masked else st

    def softmax_of(m_old, st):
        m_new = jnp.maximum(m_old, jnp.max(st, axis=0, keepdims=True))
        return m_new, jnp.exp2(m_old - m_new), jnp.exp2(st - m_new).astype(BF16)

    def pv_of(j, nblk, p, pr):
        vt = jnp.concatenate([vt_ref[j + b, lanes_of(p), :] for b in range(nblk)], axis=1)
        vaug = jnp.concatenate([vt, jnp.ones((n_one, nblk * t), BF16)], axis=0)
        return jnp.dot(vaug, pr, preferred_element_type=F32)

    def step(j, nblk, carry, masked):
        scores = [scores_of(j, nblk, p, masked) for p in range(npair)]
        probs = [softmax_of(carry[p][0], scores[p]) for p in range(npair)]
        return tuple((probs[p][0], probs[p][1] * carry[p][1] + pv_of(j, nblk, p, probs[p][2]))
                     for p in range(npair))

    init = tuple((jnp.full((1, 2 * t), -jnp.inf, F32), jnp.zeros((LANES + n_one, 2 * t), F32))
                 for _ in range(npair))
    j0 = js_ref[pl.program_id(0) * 8, i]
    n_vis = i - j0
    carry = lax.fori_loop(0, n_vis // 2, lambda jj, c: step(j0 + 2 * jj, 2, c, False), init)
    carry = lax.fori_loop(0, n_vis % 2, lambda _, c: step(i - 1, 1, c, False), carry)
    fin = step(i, 1, carry, True)
    for p in range(npair):
        acc = fin[p][1]
        ot = acc[0:LANES] / acc[LANES:LANES + 1]
        pair_t = jnp.concatenate([ot[0:half, 0:t], ot[half:LANES, t:2 * t]], axis=0)
        o_ref[:, lanes_of(p)] = pair_t.T.astype(o_ref.dtype)


def _fox(js, qt, k, vt, e, batch, lp):
    nq = lp // SEQ_TILE
    grid_spec = pltpu.PrefetchScalarGridSpec(
        num_scalar_prefetch=1,
        grid=(batch, nq),
        in_specs=[
            pl.BlockSpec((1, FOX_WIDTH, SEQ_TILE), lambda b, i, js: (b * nq + i, 0, 0)),
            pl.BlockSpec((lp, FOX_WIDTH), lambda b, i, js: (b, 0)),
            pl.BlockSpec((nq, FOX_WIDTH, SEQ_TILE), lambda b, i, js: (b, 0, 0)),
            pl.BlockSpec((lp, LANES), lambda b, i, js: (b, 0)),
        ],
        out_specs=pl.BlockSpec((SEQ_TILE, FOX_WIDTH), lambda b, i, js: (b * nq + i, 0)),
    )
    return pl.pallas_call(
        _fox_kernel,
        grid_spec=grid_spec,
        out_shape=jax.ShapeDtypeStruct((batch * lp, FOX_WIDTH), BF16),
        compiler_params=pltpu.CompilerParams(
            dimension_semantics=("arbitrary", "arbitrary"), vmem_limit_bytes=VMEM_LIMIT),
        name="fox_attention",
    )(js, qt, k, vt, e)


def _hgrn_constants():
    c = CHUNK
    m = np.zeros((N_EXP_ROWS, c), np.float32)
    upper = np.zeros((N_LEVELS, c), np.float32)
    for t in range(c):
        m[t, :t + 1] = 1.0
        m[c + t, t + 1:] = 1.0
        for lv in range(N_LEVELS):
            w = c >> (lv + 1)
            mid = (t // (2 * w)) * 2 * w + w
            if t >= mid:
                m[(2 + lv) * c + t, mid:t + 1] = 1.0
                upper[lv, t] = 1.0
            else:
                m[(2 + lv) * c + t, t + 1:mid] = 1.0
    level = np.full((c, c), N_LEVELS + 1, np.int32)
    for t in range(c):
        level[t, t] = N_LEVELS
        for s in range(t):
            w = 1 << int(np.floor(np.log2(t ^ s)))
            level[t, s] = int(np.log2(c // (2 * w)))
    m3 = np.concatenate([m, m, m], axis=1)
    return m3, upper, level


def _hgrn_kernel(hq_ref, hf_ref, hi_ref, hg_ref, lbraw_ref, nw_ref, m3_ref, up_ref, lvl_ref,
                 o_ref, st_ref, *, layer):
    @pl.when(pl.program_id(1) == 0)
    def _():
        st_ref[...] = jnp.zeros_like(st_ref)

    c = CHUNK
    raw = lbraw_ref[...]
    ex = jnp.exp(raw - jnp.max(raw, axis=0, keepdims=True))
    s_lb = ex / jnp.sum(ex, axis=0, keepdims=True)
    lb = jnp.sum(s_lb[:layer + 1], axis=0, keepdims=True) - s_lb[0:1]
    level = lvl_ref[...]
    nt = (((1,), (1,)), ((), ()))
    tn = (((0,), (0,)), ((), ()))
    n_chunks = hq_ref.shape[0] // c
    heads = [slice(h * HG_K, (h + 1) * HG_K) for h in range(HG_HEADS)]

    one_m_lb = 1.0 - lb
    floor2 = LOG_F_MIN * LOG2E
    level_is = [level == lv for lv in range(N_LEVELS + 1)]
    gates, keys, queries = [], [], []
    for ci in range(n_chunks):
        z = hf_ref[ci * c:(ci + 1) * c, :]
        omf = one_m_lb / (1.0 + jnp.exp2(LOG2E * z))
        lg2 = jnp.log2(1.0 - omf)
        gates.append(jnp.maximum(lg2, floor2))
        keys.append(jnp.where(lg2 >= floor2, omf, 1.0 - np.exp(LOG_F_MIN)))
        queries.append(_silu(hq_ref[ci * c:(ci + 1) * c, :], HG_K ** -0.5))
    exps = [jnp.dot(m3_ref[...], jnp.concatenate(_split3(g), axis=0),
                    preferred_element_type=F32) for g in gates]

    o_intra, q_dec, k_dec, dec_last = [], [], [], []
    for ci in range(n_chunks):
        q, kk, e_all = queries[ci], keys[ci], exps[ci]
        v = hi_ref[ci * c:(ci + 1) * c, :]
        b = e_all[0:c]
        q_dec.append((q * jnp.exp2(b)).astype(BF16))
        k_dec.append((kk * jnp.exp2(e_all[c:2 * c])).astype(BF16))
        dec_last.append(jnp.exp2(b[c - 1:c, :]))
        rl = [(jnp.where(up_ref[lv] > 0.5, q, kk)
               * jnp.exp2(e_all[(2 + lv) * c:(3 + lv) * c])).astype(BF16)
              for lv in range(N_LEVELS)]
        qb, kb = q.astype(BF16), kk.astype(BF16)
        per_head = []
        for hs in heads:
            a = jnp.where(level_is[N_LEVELS],
                          lax.dot_general(qb[:, hs], kb[:, hs], nt, preferred_element_type=F32),
                          0.0)
            for lv in range(N_LEVELS):
                r = rl[lv][:, hs]
                a = jnp.where(level_is[lv],
                              lax.dot_general(r, r, nt, preferred_element_type=F32), a)
            per_head.append(jnp.dot(a.astype(BF16), v[:, hs], preferred_element_type=F32))
        o_intra.append(per_head)

    states = [st_ref[h] for h in range(HG_HEADS)]
    for ci in range(n_chunks):
        rows = slice(ci * c, (ci + 1) * c)
        v = hi_ref[rows, :]
        gate = _silu(hg_ref[rows, :])
        for h, hs in enumerate(heads):
            st = states[h]
            o = o_intra[ci][h] + lax.dot_general(q_dec[ci][:, hs], st.astype(BF16), nt,
                                                 preferred_element_type=F32)
            states[h] = st * dec_last[ci][:, hs] + lax.dot_general(
                v[:, hs], k_dec[ci][:, hs], tn, preferred_element_type=F32)
            o_ref[rows, hs] = (_rms(o, nw_ref[...]) * gate[:, hs]).astype(o_ref.dtype)
    for h in range(HG_HEADS):
        st_ref[h] = states[h]


def _hgrn(hq, hf, hi, hg, lb_raw, nw, layer, batch, lp):
    nb = lp // SEQ_TILE
    m3, upper, level = _hgrn_constants()
    row_spec = pl.BlockSpec((SEQ_TILE, HG_WIDTH), lambda b, i: (b * nb + i, 0))
    full = lambda shape: pl.BlockSpec(shape, lambda b, i: (0,) * len(shape))
    depth = lb_raw.shape[0]
    return pl.pallas_call(
        functools.partial(_hgrn_kernel, layer=layer),
        grid=(batch, nb),
        in_specs=[row_spec, row_spec, row_spec, row_spec,
                  full((depth, HG_WIDTH)), _layer_spec(nw, layer),
                  full(m3.shape), full((N_LEVELS, CHUNK, 1)), full(level.shape)],
        out_specs=row_spec,
        out_shape=jax.ShapeDtypeStruct((batch * lp, HG_WIDTH), BF16),
        scratch_shapes=[pltpu.VMEM((HG_HEADS, HG_V, HG_K), F32)],
        compiler_params=pltpu.CompilerParams(
            dimension_semantics=("arbitrary", "arbitrary"), vmem_limit_bytes=VMEM_LIMIT),
        name="hgrn2",
    )(hq, hf, hi, hg, lb_raw, nw, jnp.asarray(m3, BF16),
      jnp.asarray(upper[:, :, None], F32), jnp.asarray(level))


def _mlp_kernel(head_ref, lo_ref, hi_ref, fo_lo_ref, fo_hi_ref, ho_lo_ref, ho_hi_ref, wo_ref,
                nw_ref, wg_ref, wu_ref, wd_ref, fnw_ref, out_ref, a_ref, *, first, final, g):
    if final:
        hh = pl.program_id(0) * g.nb + 1 + 2 * pl.program_id(1)
    else:
        hh = 2 * pl.program_id(0)
    h = _stream_rows(head_ref, lo_ref, hi_ref, hh, first, g)
    fo = jnp.concatenate([fo_lo_ref[0], fo_hi_ref[0]], axis=0)
    ho = jnp.concatenate([ho_lo_ref[0], ho_hi_ref[0]], axis=0)
    mixed = (jnp.dot(fo, wo_ref[0:FOX_WIDTH, :], preferred_element_type=F32)
             + jnp.dot(ho, wo_ref[FOX_WIDTH:, :], preferred_element_type=F32))
    h1 = h + mixed
    u = _rms(h1, nw_ref[...]).astype(BF16)
    dff = wg_ref.shape[1]
    for c0 in range(0, dff, FF_TILE):
        cs = slice(c0, c0 + FF_TILE)
        gate = jnp.dot(u, wg_ref[:, cs], preferred_element_type=F32)
        up = jnp.dot(u, wu_ref[:, cs], preferred_element_type=F32)
        a_ref[:, cs] = (_silu(gate) * up).astype(BF16)
    h2 = h1 + jnp.dot(a_ref[...], wd_ref[...], preferred_element_type=F32)
    if final:
        h2 = _rms(h2, fnw_ref[...])
    else:
        row = lax.broadcasted_iota(jnp.int32, (ROW_TILE, 1), 0)
        is_pad = (((row < g.pad) & (hh % g.nb == 0))
                  | ((row >= SEQ_TILE) & (row < SEQ_TILE + g.pad) & ((hh + 1) % g.nb == 0)))
        h2 = jnp.where(is_pad, 0.0, h2)
    out_ref[...] = h2


def _mlp(head, stream, first, g, layer, fo, ho, wo, nw, wg, wu, wd, fnw, final):
    d = head.shape[1]
    dff = wg.shape[-1]
    if final:
        grid = (g.batch, g.sb // 2)
        block_of = lambda b, k: b * g.nb + 1 + 2 * k
        out_rows = g.batch * g.sb * SEQ_TILE
        out_spec = pl.BlockSpec((ROW_TILE, d), lambda b, k: (b * (g.sb // 2) + k, 0))
    else:
        grid = (g.batch * g.nb // 2,)
        block_of = lambda i: 2 * i
        out_rows = g.batch * g.nb * SEQ_TILE
        out_spec = pl.BlockSpec((ROW_TILE, d), lambda i: (i, 0))
    const = lambda a: _layer_spec(a, layer, True)
    small = lambda shape: pl.BlockSpec(shape, lambda *_: (0, 0))
    blocks = lambda a: a.reshape(-1, SEQ_TILE, a.shape[-1])
    fo3, ho3 = blocks(fo), blocks(ho)
    return pl.pallas_call(
        functools.partial(_mlp_kernel, first=first, final=final, g=g),
        grid=grid,
        in_specs=[small(head.shape),
                  *_stream_specs(first, g, d, block_of),
                  *_stream_specs(False, g, FOX_WIDTH, block_of),
                  *_stream_specs(False, g, HG_WIDTH, block_of),
                  const(wo), _layer_spec(nw, layer), const(wg), const(wu), const(wd),
                  small((1, d))],
        out_specs=out_spec,
        out_shape=jax.ShapeDtypeStruct((out_rows, d), F32),
        scratch_shapes=[pltpu.VMEM((ROW_TILE, dff), BF16)],
        compiler_params=pltpu.CompilerParams(
            dimension_semantics=("arbitrary",) * len(grid), vmem_limit_bytes=VMEM_LIMIT),
        name="mlp_final" if final else "mlp",
    )(head, stream, stream, fo3, fo3, ho3, ho3, wo, nw, wg, wu, wd, fnw)


def kernel(x, meta, norm_mix_w, w_in, fox_f_bias, hgrn_lb_raw, hgrn_norm_w, w_out,
           norm_ffn_w, w_ffn_gate, w_ffn_up, w_ffn_down, norm_final_w):
    batch, seq, d = x.shape
    depth = w_in.shape[0]
    assert seq % ROW_TILE == 0 and N_META <= SEQ_TILE
    lp = seq + SEQ_TILE
    assert (batch * lp) % ROW_TILE == 0
    padf = SEQ_TILE - N_META
    g = _Geom(batch=batch, nb=lp // SEQ_TILE, sb=seq // SEQ_TILE, pad=padf)

    head = jnp.concatenate([jnp.zeros((padf, d), x.dtype), meta.astype(x.dtype)], axis=0)
    stream = x.reshape(batch * g.sb, SEQ_TILE, d)

    wfox, whg, wff = _split_w_in(w_in.astype(F32))
    wo, wg, wu, wd = (_to_bf16(w.astype(F32)) for w in (w_out, w_ffn_gate, w_ffn_up, w_ffn_down))
    fb = jnp.pad(fox_f_bias.astype(F32), ((0, 0), (0, LANES - FOX_HEADS)))[:, None]
    lb_raw = hgrn_lb_raw.astype(F32)
    nw_mix, nw_ffn, nw_hg = (a.astype(F32)[:, None]
                             for a in (norm_mix_w, norm_ffn_w, hgrn_norm_w))
    for l in range(depth):
        first, final = l == 0, l == depth - 1
        qt, k, vt, hq, hf, hi, hg, ff = _inproj(head, stream, first, g, l, nw_mix, wfox, whg, wff)
        e, js = _gates(ff, fb, l, batch, lp, padf)
        fox_o = _fox(js, qt, k, vt, e, batch, lp)
        hgrn_o = _hgrn(hq, hf, hi, hg, lb_raw, nw_hg, l, batch, lp)
        out = _mlp(head, stream, first, g, l, fox_o, hgrn_o, wo, nw_ffn, wg, wu, wd,
                   norm_final_w[None].astype(F32), final)
        stream = out.reshape(-1, SEQ_TILE, d)
    return out.reshape(batch, seq, d)
```

```python
import functools
from typing import NamedTuple

import numpy as np
import jax
import jax.numpy as jnp
from jax import lax
from jax.experimental import pallas as pl
from jax.experimental.pallas import tpu as pltpu

F32 = jnp.float32
BF16 = jnp.bfloat16

N_META = 16
FOX_HEADS = 8
FOX_HEAD_DIM = 64
FOX_WIDTH = FOX_HEADS * FOX_HEAD_DIM
HG_HEADS = 4
HG_K = 128
HG_V = 128
HG_WIDTH = HG_HEADS * HG_K
CHUNK = 64
EPS = 1e-6
MASK_VALUE = -1e30
LOG_F_MIN = -30.0
LOG2E = 1.4426950408889634
SKIP_LOG2 = 160.0
NORM_SLACK = 1.02

LANES = 128
SEQ_TILE = 256
ROW_TILE = 512
FF_TILE = 256
VMEM_LIMIT = 56 * 1024 * 1024

N_LEVELS = 6
N_EXP_ROWS = (2 + N_LEVELS) * CHUNK


def _rms(x, w):
    ms = jnp.mean(x * x, axis=-1, keepdims=True)
    return x * lax.rsqrt(ms + EPS) * w


def _split3(x):
    hi = x.astype(BF16)
    r1 = x - hi.astype(F32)
    mid = r1.astype(BF16)
    lo = (r1 - mid.astype(F32)).astype(BF16)
    return hi, mid, lo


def _silu(x, scale=1.0):
    return (scale * x) / (1.0 + jnp.exp2(-LOG2E * x))


CAST_ROWS = 256


def _cast_kernel(w_ref, o_ref):
    o_ref[...] = w_ref[...].astype(BF16)


def _to_bf16(w):
    depth, kdim, n = w.shape
    spec = pl.BlockSpec((1, CAST_ROWS, n), lambda l, r: (l, r, 0))
    return pl.pallas_call(
        _cast_kernel,
        grid=(depth, kdim // CAST_ROWS),
        in_specs=[spec],
        out_specs=spec,
        out_shape=jax.ShapeDtypeStruct(w.shape, BF16),
        compiler_params=pltpu.CompilerParams(
            dimension_semantics=("arbitrary", "arbitrary"), vmem_limit_bytes=VMEM_LIMIT),
        name="cast_weights",
    )(w)


def _split_w_in_kernel(w_ref, fox_ref, hg_ref, ff_ref):
    fw = 3 * FOX_WIDTH
    fox_ref[...] = w_ref[:, :, 0:fw].astype(BF16)
    hg_ref[...] = w_ref[:, :, fw + FOX_HEADS:].astype(BF16)
    blk = w_ref[:, :, fw:fw + LANES]
    lane = lax.broadcasted_iota(jnp.int32, blk.shape, 2)
    ff_ref[...] = jnp.where(lane < FOX_HEADS, blk, 0.0).astype(BF16)


def _split_w_in(w):
    depth, kdim, n = w.shape
    fw = 3 * FOX_WIDTH
    widths = (fw, n - fw - FOX_HEADS, LANES)
    spec = lambda c: pl.BlockSpec((1, CAST_ROWS, c), lambda l, r: (l, r, 0))
    return pl.pallas_call(
        _split_w_in_kernel,
        grid=(depth, kdim // CAST_ROWS),
        in_specs=[spec(n)],
        out_specs=tuple(spec(c) for c in widths),
        out_shape=tuple(jax.ShapeDtypeStruct((depth, kdim, c), BF16) for c in widths),
        compiler_params=pltpu.CompilerParams(
            dimension_semantics=("arbitrary", "arbitrary"), vmem_limit_bytes=VMEM_LIMIT),
        name="split_w_in",
    )(w)


def _layer_spec(a, layer, resident=False):
    tail = a.shape[1:]
    index = lambda *_: (layer,) + (0,) * len(tail)
    if resident:
        return pl.BlockSpec((None,) + tail, index, pipeline_mode=pl.Buffered(1))
    return pl.BlockSpec((None,) + tail, index)


class _Geom(NamedTuple):
    batch: int
    nb: int
    sb: int
    pad: int


def _stream_specs(first, g, d, block_of):
    def spec(half):
        def index(*idx):
            hh = block_of(*idx) + half
            if first:
                hh = (hh // g.nb) * g.sb + jnp.maximum(hh % g.nb - 1, 0)
            return (hh, 0, 0)
        return pl.BlockSpec((1, SEQ_TILE, d), index)
    return [spec(0), spec(1)]


def _stream_rows(head_ref, lo_ref, hi_ref, hh, first, g):
    lo, hi = lo_ref[0], hi_ref[0]
    if first:
        lo = jnp.where(hh % g.nb == 0, head_ref[...], lo)
        hi = jnp.where((hh + 1) % g.nb == 0, head_ref[...], hi)
    return jnp.concatenate([lo, hi], axis=0)


def _inproj_kernel(head_ref, lo_ref, hi_ref, nw_ref, wfox_ref, whg_ref, wff_ref, qt_ref, k_ref,
                   vt_ref, hq_ref, hf_ref, hi_out_ref, hg_ref, ff_ref, *, first, g):
    h = _stream_rows(head_ref, lo_ref, hi_ref, 2 * pl.program_id(0), first, g)
    u = _rms(h, nw_ref[...]).astype(BF16)

    def mm(c0, c1, w_ref=wfox_ref):
        return jnp.dot(u, w_ref[:, c0:c1], preferred_element_type=F32)

    w = FOX_WIDTH
    q = mm(0, w) * (FOX_HEAD_DIM ** -0.5 * LOG2E)
    v = mm(2 * w, 3 * w)
    for s in range(ROW_TILE // SEQ_TILE):
        rows = slice(s * SEQ_TILE, (s + 1) * SEQ_TILE)
        qt_ref[s] = q[rows].T.astype(BF16)
        vt_ref[s] = v[rows].T.astype(BF16)
    hq_ref[...] = mm(0, HG_WIDTH, whg_ref)
    hf_ref[...] = mm(HG_WIDTH, 2 * HG_WIDTH, whg_ref)
    hi_out_ref[...] = mm(2 * HG_WIDTH, 3 * HG_WIDTH, whg_ref).astype(BF16)
    hg_ref[...] = mm(3 * HG_WIDTH, 4 * HG_WIDTH, whg_ref)
    kf = mm(w, 2 * w)
    k_ref[...] = kf.astype(BF16)
    col = lax.broadcasted_iota(jnp.int32, (w, LANES), 0)
    lane = lax.broadcasted_iota(jnp.int32, (w, LANES), 1)
    head = jnp.right_shift(col, int(np.log2(FOX_HEAD_DIM)))
    sel_q = (lane == FOX_HEADS + head).astype(BF16)
    sel_k = (lane == 2 * FOX_HEADS + head).astype(BF16)
    ff_ref[...] = (mm(0, LANES, wff_ref)
                   + jnp.dot((q * q).astype(BF16), sel_q, preferred_element_type=F32)
                   + jnp.dot((kf * kf).astype(BF16), sel_k, preferred_element_type=F32))


def _inproj(head, stream, first, g, layer, nw, wfox, whg, wff):
    d = head.shape[1]
    rows = g.batch * g.nb * SEQ_TILE
    grid = (rows // ROW_TILE,)
    row_spec = lambda c: pl.BlockSpec((ROW_TILE, c), lambda i: (i, 0))
    slabs = ROW_TILE // SEQ_TILE
    slab_shape = jax.ShapeDtypeStruct((rows // SEQ_TILE, FOX_WIDTH, SEQ_TILE), BF16)
    slab_spec = pl.BlockSpec((slabs, FOX_WIDTH, SEQ_TILE), lambda i: (i, 0, 0))
    out_shape = (
        slab_shape,
        jax.ShapeDtypeStruct((rows, FOX_WIDTH), BF16),
        slab_shape,
        jax.ShapeDtypeStruct((rows, HG_WIDTH), F32),
        jax.ShapeDtypeStruct((rows, HG_WIDTH), F32),
        jax.ShapeDtypeStruct((rows, HG_WIDTH), BF16),
        jax.ShapeDtypeStruct((rows, HG_WIDTH), F32),
        jax.ShapeDtypeStruct((rows, LANES), F32),
    )
    return pl.pallas_call(
        functools.partial(_inproj_kernel, first=first, g=g),
        grid=grid,
        in_specs=[
            pl.BlockSpec(head.shape, lambda i: (0, 0)),
            *_stream_specs(first, g, d, lambda i: 2 * i),
            _layer_spec(nw, layer),
            _layer_spec(wfox, layer, True), _layer_spec(whg, layer, True),
            _layer_spec(wff, layer, True),
        ],
        out_specs=tuple(slab_spec if len(s.shape) == 3 else row_spec(s.shape[1])
                        for s in out_shape),
        out_shape=out_shape,
        compiler_params=pltpu.CompilerParams(
            dimension_semantics=("arbitrary",), vmem_limit_bytes=VMEM_LIMIT),
        name="inproj",
    )(head, stream, stream, nw, wfox, whg, wff)


def _gates_kernel(ff_ref, fb_ref, e_ref, js_ref, *, padf):
    lp = ff_ref.shape[0]
    t = SEQ_TILE
    nblk = lp // t
    nrow = -(-nblk // 8) * 8
    h8 = FOX_HEADS
    row = lax.broadcasted_iota(jnp.int32, (t, t), 0)
    col = lax.broadcasted_iota(jnp.int32, (t, t), 1)
    tri = (row >= col).astype(BF16)
    lane = lax.broadcasted_iota(jnp.int32, (t, LANES), 1)
    rloc = lax.broadcasted_iota(jnp.int32, (t, LANES), 0)
    brow = lax.broadcasted_iota(jnp.int32, (nrow, LANES), 0)
    blane = lax.broadcasted_iota(jnp.int32, (nrow, LANES), 1)
    carry = jnp.zeros((1, LANES), F32)
    bmax = jnp.zeros((nrow, LANES), F32)
    bmin = jnp.zeros((nrow, LANES), F32)
    qnorm = jnp.zeros((nrow, LANES), F32)
    knorm = jnp.zeros((nrow, LANES), F32)
    for blk in range(nblk):
        raw = ff_ref[blk * t:(blk + 1) * t, :]
        z = raw + fb_ref[...]
        logf = jnp.minimum(z, 0.0) - jnp.log1p(jnp.exp(-jnp.abs(z)))
        hi, mid, lo = _split3(logf)
        c = (jnp.dot(tri, hi, preferred_element_type=F32)
             + jnp.dot(tri, mid, preferred_element_type=F32)
             + jnp.dot(tri, lo, preferred_element_type=F32)) + carry
        carry = c[t - 1:t, :]
        bias = jnp.where(rloc + blk * t >= padf, -LOG2E * c, MASK_VALUE)
        bh, bm, bl = _split3(bias)
        packed = jnp.where(
            lane < h8, bh.astype(F32),
            jnp.where(lane < 2 * h8, pltpu.roll(bm.astype(F32), h8, 1),
                      jnp.where(lane < 3 * h8, pltpu.roll(bl.astype(F32), 2 * h8, 1), 0.0)))
        e_ref[blk * t:(blk + 1) * t, :] = packed.astype(BF16)

        here = brow == blk
        sq = jnp.max(raw, axis=0, keepdims=True) * NORM_SLACK
        bmax = jnp.where(here, jnp.max(bias, axis=0, keepdims=True), bmax)
        bmin = jnp.where(here, jnp.min(bias, axis=0, keepdims=True), bmin)
        qnorm = jnp.where(here, jnp.sqrt(pltpu.roll(sq, LANES - h8, 1)), qnorm)
        knorm = jnp.where(here, jnp.sqrt(pltpu.roll(sq, LANES - 2 * h8, 1)), knorm)

    jlane = lax.broadcasted_iota(jnp.int32, js_ref.shape, 1)
    js = jnp.zeros(js_ref.shape, jnp.int32)
    rowf = brow[:, 0:1].astype(F32)
    for i in range(nblk):
        bound = qnorm[i:i + 1] * (knorm + knorm[i:i + 1]) + bmax - bmin[i:i + 1]
        needed = jnp.where((blane < h8) & (bound > -SKIP_LOG2), 1.0, 0.0)
        needed = jnp.max(needed, axis=1, keepdims=True)
        first = jnp.min(jnp.where((needed > 0.0) | (rowf >= i), rowf, float(nrow)),
                        axis=0, keepdims=True)
        js = jnp.where(jlane == i, first.astype(jnp.int32), js)
    js_ref[...] = js


def _gates(ff, fb, layer, batch, lp, padf):
    return pl.pallas_call(
        functools.partial(_gates_kernel, padf=padf),
        grid=(batch,),
        in_specs=[pl.BlockSpec((lp, LANES), lambda b: (b, 0)),
                  _layer_spec(fb, layer)],
        out_specs=(pl.BlockSpec((lp, LANES), lambda b: (b, 0)),
                   pl.BlockSpec((8, LANES), lambda b: (b, 0))),
        out_shape=(jax.ShapeDtypeStruct((batch * lp, LANES), BF16),
                   jax.ShapeDtypeStruct((batch * 8, LANES), jnp.int32)),
        compiler_params=pltpu.CompilerParams(
            dimension_semantics=("arbitrary",), vmem_limit_bytes=VMEM_LIMIT),
        name="fox_gates",
    )(ff, fb)


def _fox_kernel(js_ref, qt_ref, k_ref, vt_ref, e_ref, o_ref):
    i = pl.program_id(1)
    t = SEQ_TILE
    half = FOX_HEAD_DIM
    npair = FOX_HEADS // 2
    n_one = 16
    srow = lax.broadcasted_iota(jnp.int32, (LANES, t), 0)
    key_i = lax.broadcasted_iota(jnp.int32, (t, 2 * t), 0)
    qry_i = lax.broadcasted_iota(jnp.int32, (t, 2 * t), 1)
    causal = key_i <= jnp.where(qry_i >= t, qry_i - t, qry_i)
    lanes_of = lambda p: slice(p * LANES, (p + 1) * LANES)

    def stacked_qt(p):
        top = qt_ref[0, lanes_of(p), :]
        zero = jnp.zeros_like(top)
        sel_a = ((srow == 2 * p) | (srow == FOX_HEADS + 2 * p)
                 | (srow == 2 * FOX_HEADS + 2 * p)).astype(BF16)
        sel_b = ((srow == 2 * p + 1) | (srow == FOX_HEADS + 2 * p + 1)
                 | (srow == 2 * FOX_HEADS + 2 * p + 1)).astype(BF16)
        col_a = jnp.concatenate([jnp.where(srow < half, top, zero), sel_a], axis=0)
        col_b = jnp.concatenate([jnp.where(srow >= half, top, zero), sel_b], axis=0)
        return jnp.concatenate([col_a, col_b], axis=1)

    qqt = [stacked_qt(p) for p in range(npair)]

    def scores_of(j, nblk, p, masked):
        rows = pl.ds(pl.multiple_of(j * t, t), nblk * t)
        kaug = jnp.concatenate([k_ref[rows, lanes_of(p)], e_ref[rows, :]], axis=1)
        st = jnp.dot(kaug, qqt[p], preferred_element_type=F32)
        return jnp.where(causal, st, MASK_VALUE) if masked else st

    def softmax_of(m_old, st):
        m_new = jnp.maximum(m_old, jnp.max(st, axis=0, keepdims=True))
        return m_new, jnp.exp2(m_old - m_new), jnp.exp2(st - m_new).astype(BF16)

    def pv_of(j, nblk, p, pr):
        vt = jnp.concatenate([vt_ref[j + b, lanes_of(p), :] for b in range(nblk)], axis=1)
        vaug = jnp.concatenate([vt, jnp.ones((n_one, nblk * t), BF16)], axis=0)
        return jnp.dot(vaug, pr, preferred_element_type=F32)

    def step(j, nblk, carry, masked):
        scores = [scores_of(j, nblk, p, masked) for p in range(npair)]
        probs = [softmax_of(carry[p][0], scores[p]) for p in range(npair)]
        return tuple((probs[p][0], probs[p][1] * carry[p][1] + pv_of(j, nblk, p, probs[p][2]))
                     for p in range(npair))

    init = tuple((jnp.full((1, 2 * t), -jnp.inf, F32), jnp.zeros((LANES + n_one, 2 * t), F32))
                 for _ in range(npair))
    j0 = js_ref[pl.program_id(0) * 8, i]
    n_vis = i - j0
    carry = lax.fori_loop(0, n_vis // 2, lambda jj, c: step(j0 + 2 * jj, 2, c, False), init)
    carry = lax.fori_loop(0, n_vis % 2, lambda _, c: step(i - 1, 1, c, False), carry)
    fin = step(i, 1, carry, True)
    for p in range(npair):
        acc = fin[p][1]
        ot = acc[0:LANES] / acc[LANES:LANES + 1]
        pair_t = jnp.concatenate([ot[0:half, 0:t], ot[half:LANES, t:2 * t]], axis=0)
        o_ref[:, lanes_of(p)] = pair_t.T.astype(o_ref.dtype)


def _fox(js, qt, k, vt, e, batch, lp):
    nq = lp // SEQ_TILE
    grid_spec = pltpu.PrefetchScalarGridSpec(
        num_scalar_prefetch=1,
        grid=(batch, nq),
        in_specs=[
            pl.BlockSpec((1, FOX_WIDTH, SEQ_TILE), lambda b, i, js: (b * nq + i, 0, 0)),
            pl.BlockSpec((lp, FOX_WIDTH), lambda b, i, js: (b, 0)),
            pl.BlockSpec((nq, FOX_WIDTH, SEQ_TILE), lambda b, i, js: (b, 0, 0)),
            pl.BlockSpec((lp, LANES), lambda b, i, js: (b, 0)),
        ],
        out_specs=pl.BlockSpec((SEQ_TILE, FOX_WIDTH), lambda b, i, js: (b * nq + i, 0)),
    )
    return pl.pallas_call(
        _fox_kernel,
        grid_spec=grid_spec,
        out_shape=jax.ShapeDtypeStruct((batch * lp, FOX_WIDTH), BF16),
        compiler_params=pltpu.CompilerParams(
            dimension_semantics=("arbitrary", "arbitrary"), vmem_limit_bytes=VMEM_LIMIT),
        name="fox_attention",
    )(js, qt, k, vt, e)


def _hgrn_constants():
    c = CHUNK
    m = np.zeros((N_EXP_ROWS, c), np.float32)
    upper = np.zeros((N_LEVELS, c), np.float32)
    for t in range(c):
        m[t, :t + 1] = 1.0
        m[c + t, t + 1:] = 1.0
        for lv in range(N_LEVELS):
            w = c >> (lv + 1)
            mid = (t // (2 * w)) * 2 * w + w
            if t >= mid:
                m[(2 + lv) * c + t, mid:t + 1] = 1.0
                upper[lv, t] = 1.0
            else:
                m[(2 + lv) * c + t, t + 1:mid] = 1.0
    level = np.full((c, c), N_LEVELS + 1, np.int32)
    for t in range(c):
        level[t, t] = N_LEVELS
        for s in range(t):
            w = 1 << int(np.floor(np.log2(t ^ s)))
            level[t, s] = int(np.log2(c // (2 * w)))
    m3 = np.concatenate([m, m, m], axis=1)
    return m3, upper, level


def _hgrn_kernel(hq_ref, hf_ref, hi_ref, hg_ref, lbraw_ref, nw_ref, m3_ref, up_ref, lvl_ref,
                 o_ref, st_ref, *, layer):
    @pl.when(pl.program_id(1) == 0)
    def _():
        st_ref[...] = jnp.zeros_like(st_ref)

    c = CHUNK
    raw = lbraw_ref[...]
    ex = jnp.exp(raw - jnp.max(raw, axis=0, keepdims=True))
    s_lb = ex / jnp.sum(ex, axis=0, keepdims=True)
    lb = jnp.sum(s_lb[:layer + 1], axis=0, keepdims=True) - s_lb[0:1]
    level = lvl_ref[...]
    nt = (((1,), (1,)), ((), ()))
    tn = (((0,), (0,)), ((), ()))
    n_chunks = hq_ref.shape[0] // c
    heads = [slice(h * HG_K, (h + 1) * HG_K) for h in range(HG_HEADS)]

    one_m_lb = 1.0 - lb
    floor2 = LOG_F_MIN * LOG2E
    level_is = [level == lv for lv in range(N_LEVELS + 1)]
    gates, keys, queries = [], [], []
    for ci in range(n_chunks):
        z = hf_ref[ci * c:(ci + 1) * c, :]
        omf = one_m_lb / (1.0 + jnp.exp2(LOG2E * z))
        lg2 = jnp.log2(1.0 - omf)
        gates.append(jnp.maximum(lg2, floor2))
        keys.append(jnp.where(lg2 >= floor2, omf, 1.0 - np.exp(LOG_F_MIN)))
        queries.append(_silu(hq_ref[ci * c:(ci + 1) * c, :], HG_K ** -0.5))
    exps = [jnp.dot(m3_ref[...], jnp.concatenate(_split3(g), axis=0),
                    preferred_element_type=F32) for g in gates]

    o_intra, q_dec, k_dec, dec_last = [], [], [], []
    for ci in range(n_chunks):
        q, kk, e_all = queries[ci], keys[ci], exps[ci]
        v = hi_ref[ci * c:(ci + 1) * c, :]
        b = e_all[0:c]
        q_dec.append((q * jnp.exp2(b)).astype(BF16))
        k_dec.append((kk * jnp.exp2(e_all[c:2 * c])).astype(BF16))
        dec_last.append(jnp.exp2(b[c - 1:c, :]))
        rl = [(jnp.where(up_ref[lv] > 0.5, q, kk)
               * jnp.exp2(e_all[(2 + lv) * c:(3 + lv) * c])).astype(BF16)
              for lv in range(N_LEVELS)]
        qb, kb = q.astype(BF16), kk.astype(BF16)
        per_head = []
        for hs in heads:
            a = jnp.where(level_is[N_LEVELS],
                          lax.dot_general(qb[:, hs], kb[:, hs], nt, preferred_element_type=F32),
                          0.0)
            for lv in range(N_LEVELS):
                r = rl[lv][:, hs]
                a = jnp.where(level_is[lv],
                              lax.dot_general(r, r, nt, preferred_element_type=F32), a)
            per_head.append(jnp.dot(a.astype(BF16), v[:, hs], preferred_element_type=F32))
        o_intra.append(per_head)

    states = [st_ref[h] for h in range(HG_HEADS)]
    for ci in range(n_chunks):
        rows = slice(ci * c, (ci + 1) * c)
        v = hi_ref[rows, :]
        gate = _silu(hg_ref[rows, :])
        for h, hs in enumerate(heads):
            st = states[h]
            o = o_intra[ci][h] + lax.dot_general(q_dec[ci][:, hs], st.astype(BF16), nt,
                                                 preferred_element_type=F32)
            states[h] = st * dec_last[ci][:, hs] + lax.dot_general(
                v[:, hs], k_dec[ci][:, hs], tn, preferred_element_type=F32)
            o_ref[rows, hs] = (_rms(o, nw_ref[...]) * gate[:, hs]).astype(o_ref.dtype)
    for h in range(HG_HEADS):
        st_ref[h] = states[h]


def _hgrn(hq, hf, hi, hg, lb_raw, nw, layer, batch, lp):
    nb = lp // SEQ_TILE
    m3, upper, level = _hgrn_constants()
    row_spec = pl.BlockSpec((SEQ_TILE, HG_WIDTH), lambda b, i: (b * nb + i, 0))
    full = lambda shape: pl.BlockSpec(shape, lambda b, i: (0,) * len(shape))
    depth = lb_raw.shape[0]
    return pl.pallas_call(
        functools.partial(_hgrn_kernel, layer=layer),
        grid=(batch, nb),
        in_specs=[row_spec, row_spec, row_spec, row_spec,
                  full((depth, HG_WIDTH)), _layer_spec(nw, layer),
                  full(m3.shape), full((N_LEVELS, CHUNK, 1)), full(level.shape)],
        out_specs=row_spec,
        out_shape=jax.ShapeDtypeStruct((batch * lp, HG_WIDTH), BF16),
        scratch_shapes=[pltpu.VMEM((HG_HEADS, HG_V, HG_K), F32)],
        compiler_params=pltpu.CompilerParams(
            dimension_semantics=("arbitrary", "arbitrary"), vmem_limit_bytes=VMEM_LIMIT),
        name="hgrn2",
    )(hq, hf, hi, hg, lb_raw, nw, jnp.asarray(m3, BF16),
      jnp.asarray(upper[:, :, None], F32), jnp.asarray(level))


def _mlp_kernel(head_ref, lo_ref, hi_ref, fo_lo_ref, fo_hi_ref, ho_lo_ref, ho_hi_ref, wo_ref,
                nw_ref, wg_ref, wu_ref, wd_ref, fnw_ref, out_ref, a_ref, *, first, final, g):
    if final:
        hh = pl.program_id(0) * g.nb + 1 + 2 * pl.program_id(1)
    else:
        hh = 2 * pl.program_id(0)
    h = _stream_rows(head_ref, lo_ref, hi_ref, hh, first, g)
    fo = jnp.concatenate([fo_lo_ref[0], fo_hi_ref[0]], axis=0)
    ho = jnp.concatenate([ho_lo_ref[0], ho_hi_ref[0]], axis=0)
    mixed = (jnp.dot(fo, wo_ref[0:FOX_WIDTH, :], preferred_element_type=F32)
             + jnp.dot(ho, wo_ref[FOX_WIDTH:, :], preferred_element_type=F32))
    h1 = h + mixed
    u = _rms(h1, nw_ref[...])
    dff = wg_ref.shape[1]
    for c0 in range(0, dff, FF_TILE):
        cs = slice(c0, c0 + FF_TILE)
        gate = jnp.dot(u, wg_ref[:, cs], preferred_element_type=F32)
        up = jnp.dot(u, wu_ref[:, cs], preferred_element_type=F32)
        a_ref[:, cs] = (_silu(gate) * up).astype(BF16)
    h2 = h1 + jnp.dot(a_ref[...], wd_ref[...], preferred_element_type=F32)
    if final:
        h2 = _rms(h2, fnw_ref[...])
    else:
        row = lax.broadcasted_iota(jnp.int32, (ROW_TILE, 1), 0)
        is_pad = (((row < g.pad) & (hh % g.nb == 0))
                  | ((row >= SEQ_TILE) & (row < SEQ_TILE + g.pad) & ((hh + 1) % g.nb == 0)))
        h2 = jnp.where(is_pad, 0.0, h2)
    out_ref[...] = h2


def _mlp(head, stream, first, g, layer, fo, ho, wo, nw, wg, wu, wd, fnw, final):
    d = head.shape[1]
    dff = wg.shape[-1]
    if final:
        grid = (g.batch, g.sb // 2)
        block_of = lambda b, k: b * g.nb + 1 + 2 * k
        out_rows = g.batch * g.sb * SEQ_TILE
        out_spec = pl.BlockSpec((ROW_TILE, d), lambda b, k: (b * (g.sb // 2) + k, 0))
    else:
        grid = (g.batch * g.nb // 2,)
        block_of = lambda i: 2 * i
        out_rows = g.batch * g.nb * SEQ_TILE
        out_spec = pl.BlockSpec((ROW_TILE, d), lambda i: (i, 0))
    const = lambda a: _layer_spec(a, layer, True)
    small = lambda shape: pl.BlockSpec(shape, lambda *_: (0, 0))
    blocks = lambda a: a.reshape(-1, SEQ_TILE, a.shape[-1])
    fo3, ho3 = blocks(fo), blocks(ho)
    return pl.pallas_call(
        functools.partial(_mlp_kernel, first=first, final=final, g=g),
        grid=grid,
        in_specs=[small(head.shape),
                  *_stream_specs(first, g, d, block_of),
                  *_stream_specs(False, g, FOX_WIDTH, block_of),
                  *_stream_specs(False, g, HG_WIDTH, block_of),
                  const(wo), _layer_spec(nw, layer), const(wg), const(wu), const(wd),
                  small((1, d))],
        out_specs=out_spec,
        out_shape=jax.ShapeDtypeStruct((out_rows, d), F32),
        scratch_shapes=[pltpu.VMEM((ROW_TILE, dff), BF16)],
        compiler_params=pltpu.CompilerParams(
            dimension_semantics=("arbitrary",) * len(grid), vmem_limit_bytes=VMEM_LIMIT),
        name="mlp_final" if final else "mlp",
    )(head, stream, stream, fo3, fo3, ho3, ho3, wo, nw, wg, wu, wd, fnw)


def kernel(x, meta, norm_mix_w, w_in, fox_f_bias, hgrn_lb_raw, hgrn_norm_w, w_out,
           norm_ffn_w, w_ffn_gate, w_ffn_up, w_ffn_down, norm_final_w):
    batch, seq, d = x.shape
    depth = w_in.shape[0]
    assert seq % ROW_TILE == 0 and N_META <= SEQ_TILE
    lp = seq + SEQ_TILE
    assert (batch * lp) % ROW_TILE == 0
    padf = SEQ_TILE - N_META
    g = _Geom(batch=batch, nb=lp // SEQ_TILE, sb=seq // SEQ_TILE, pad=padf)

    head = jnp.concatenate([jnp.zeros((padf, d), x.dtype), meta.astype(x.dtype)], axis=0)
    stream = x.reshape(batch * g.sb, SEQ_TILE, d)

    wfox, whg, wff = _split_w_in(w_in.astype(F32))
    wo, wg, wu, wd = (w.astype(F32) for w in (w_out, w_ffn_gate, w_ffn_up, w_ffn_down))
    fb = jnp.pad(fox_f_bias.astype(F32), ((0, 0), (0, LANES - FOX_HEADS)))[:, None]
    lb_raw = hgrn_lb_raw.astype(F32)
    nw_mix, nw_ffn, nw_hg = (a.astype(F32)[:, None]
                             for a in (norm_mix_w, norm_ffn_w, hgrn_norm_w))
    for l in range(depth):
        first, final = l == 0, l == depth - 1
        qt, k, vt, hq, hf, hi, hg, ff = _inproj(head, stream, first, g, l, nw_mix, wfox, whg, wff)
        e, js = _gates(ff, fb, l, batch, lp, padf)
        fox_o = _fox(js, qt, k, vt, e, batch, lp)
        hgrn_o = _hgrn(hq, hf, hi, hg, lb_raw, nw_hg, l, batch, lp)
        out = _mlp(head, stream, first, g, l, fox_o, hgrn_o, wo, nw_ffn, wg, wu, wd,
                   norm_final_w[None].astype(F32), final)
        stream = out.reshape(-1, SEQ_TILE, d)
    return out.reshape(batch, seq, d)
```

```python
import functools
from typing import NamedTuple

import numpy as np
import jax
import jax.numpy as jnp
from jax import lax
from jax.experimental import pallas as pl
from jax.experimental.pallas import tpu as pltpu

F32 = jnp.float32
BF16 = jnp.bfloat16

N_META = 16
FOX_HEADS = 8
FOX_HEAD_DIM = 64
FOX_WIDTH = FOX_HEADS * FOX_HEAD_DIM
HG_HEADS = 4
HG_K = 128
HG_V = 128
HG_WIDTH = HG_HEADS * HG_K
CHUNK = 64
EPS = 1e-6
MASK_VALUE = -1e30
LOG_F_MIN = -30.0
LOG2E = 1.4426950408889634
SKIP_LOG2 = 160.0
NORM_SLACK = 1.02

LANES = 128
SEQ_TILE = 256
ROW_TILE = 512
FF_TILE = 256
VMEM_LIMIT = 56 * 1024 * 1024

N_LEVELS = 6
N_EXP_ROWS = (2 + N_LEVELS) * CHUNK


def _rms(x, w):
    ms = jnp.mean(x * x, axis=-1, keepdims=True)
    return x * lax.rsqrt(ms + EPS) * w


def _split3(x):
    hi = x.astype(BF16)
    r1 = x - hi.astype(F32)
    mid = r1.astype(BF16)
    lo = (r1 - mid.astype(F32)).astype(BF16)
    return hi, mid, lo


def _silu(x, scale=1.0):
    return (scale * x) / (1.0 + jnp.exp2(-LOG2E * x))


def _split_w_in_kernel(wt_ref, fox_ref, hg_ref, ff_ref):
    fw = 3 * FOX_WIDTH
    t = SEQ_TILE

    def block(r0, rows=t):
        return wt_ref[r0:r0 + rows, :]

    for j in range(fw // t):
        fox_ref[:, j * t:(j + 1) * t] = block(j * t).T.astype(BF16)
    for j in range(hg_ref.shape[1] // t):
        hg_ref[:, j * t:(j + 1) * t] = block(fw + FOX_HEADS + j * t).T.astype(BF16)
    ffb = block(fw, LANES)
    row = lax.broadcasted_iota(jnp.int32, ffb.shape, 0)
    ff_ref[...] = jnp.where(row < FOX_HEADS, ffb, 0.0).T.astype(BF16)


def _split_w_in(w):
    depth, kdim, n = w.shape
    fw = 3 * FOX_WIDTH
    widths = (fw, n - fw - FOX_HEADS, LANES)
    wt = jnp.transpose(w, (0, 2, 1))
    return pl.pallas_call(
        _split_w_in_kernel,
        grid=(depth,),
        in_specs=[pl.BlockSpec((None, n, kdim), lambda l: (l, 0, 0))],
        out_specs=tuple(pl.BlockSpec((None, kdim, c), lambda l: (l, 0, 0)) for c in widths),
        out_shape=tuple(jax.ShapeDtypeStruct((depth, kdim, c), BF16) for c in widths),
        compiler_params=pltpu.CompilerParams(
            dimension_semantics=("arbitrary",), vmem_limit_bytes=VMEM_LIMIT),
        name="split_w_in",
    )(wt)


def _layer_spec(a, layer, resident=False):
    tail = a.shape[1:]
    index = lambda *_: (layer,) + (0,) * len(tail)
    if resident:
        return pl.BlockSpec((None,) + tail, index, pipeline_mode=pl.Buffered(1))
    return pl.BlockSpec((None,) + tail, index)


class _Geom(NamedTuple):
    batch: int
    nb: int
    sb: int
    pad: int


def _stream_specs(first, g, d, block_of):
    def spec(half):
        def index(*idx):
            hh = block_of(*idx) + half
            if first:
                hh = (hh // g.nb) * g.sb + jnp.maximum(hh % g.nb - 1, 0)
            return (hh, 0, 0)
        return pl.BlockSpec((1, SEQ_TILE, d), index)
    return [spec(0), spec(1)]


def _stream_rows(head_ref, lo_ref, hi_ref, hh, first, g):
    lo, hi = lo_ref[0], hi_ref[0]
    if first:
        lo = jnp.where(hh % g.nb == 0, head_ref[...], lo)
        hi = jnp.where((hh + 1) % g.nb == 0, head_ref[...], hi)
    return jnp.concatenate([lo, hi], axis=0)


def _inproj_kernel(head_ref, lo_ref, hi_ref, nw_ref, wfox_ref, whg_ref, wff_ref, qt_ref, k_ref,
                   vt_ref, hq_ref, hf_ref, hi_out_ref, hg_ref, ff_ref, *, first, g):
    h = _stream_rows(head_ref, lo_ref, hi_ref, 2 * pl.program_id(0), first, g)
    u = _rms(h, nw_ref[...]).astype(BF16)

    def mm(c0, c1, w_ref=wfox_ref):
        return jnp.dot(u, w_ref[:, c0:c1], preferred_element_type=F32)

    w = FOX_WIDTH
    q = mm(0, w) * (FOX_HEAD_DIM ** -0.5 * LOG2E)
    v = mm(2 * w, 3 * w)
    for s in range(ROW_TILE // SEQ_TILE):
        rows = slice(s * SEQ_TILE, (s + 1) * SEQ_TILE)
        qt_ref[s] = q[rows].T.astype(BF16)
        vt_ref[s] = v[rows].T.astype(BF16)
    hq_ref[...] = mm(0, HG_WIDTH, whg_ref)
    hf_ref[...] = mm(HG_WIDTH, 2 * HG_WIDTH, whg_ref)
    hi_out_ref[...] = mm(2 * HG_WIDTH, 3 * HG_WIDTH, whg_ref).astype(BF16)
    hg_ref[...] = mm(3 * HG_WIDTH, 4 * HG_WIDTH, whg_ref)
    kf = mm(w, 2 * w)
    k_ref[...] = kf.astype(BF16)
    col = lax.broadcasted_iota(jnp.int32, (w, LANES), 0)
    lane = lax.broadcasted_iota(jnp.int32, (w, LANES), 1)
    head = jnp.right_shift(col, int(np.log2(FOX_HEAD_DIM)))
    sel_q = (lane == FOX_HEADS + head).astype(BF16)
    sel_k = (lane == 2 * FOX_HEADS + head).astype(BF16)
    ff_ref[...] = (mm(0, LANES, wff_ref)
                   + jnp.dot((q * q).astype(BF16), sel_q, preferred_element_type=F32)
                   + jnp.dot((kf * kf).astype(BF16), sel_k, preferred_element_type=F32))


def _inproj(head, stream, first, g, layer, nw, wfox, whg, wff):
    d = head.shape[1]
    rows = g.batch * g.nb * SEQ_TILE
    grid = (rows // ROW_TILE,)
    row_spec = lambda c: pl.BlockSpec((ROW_TILE, c), lambda i: (i, 0))
    slabs = ROW_TILE // SEQ_TILE
    slab_shape = jax.ShapeDtypeStruct((rows // SEQ_TILE, FOX_WIDTH, SEQ_TILE), BF16)
    slab_spec = pl.BlockSpec((slabs, FOX_WIDTH, SEQ_TILE), lambda i: (i, 0, 0))
    out_shape = (
        slab_shape,
        jax.ShapeDtypeStruct((rows, FOX_WIDTH), BF16),
        slab_shape,
        jax.ShapeDtypeStruct((rows, HG_WIDTH), F32),
        jax.ShapeDtypeStruct((rows, HG_WIDTH), F32),
        jax.ShapeDtypeStruct((rows, HG_WIDTH), BF16),
        jax.ShapeDtypeStruct((rows, HG_WIDTH), F32),
        jax.ShapeDtypeStruct((rows, LANES), F32),
    )
    return pl.pallas_call(
        functools.partial(_inproj_kernel, first=first, g=g),
        grid=grid,
        in_specs=[
            pl.BlockSpec(head.shape, lambda i: (0, 0)),
            *_stream_specs(first, g, d, lambda i: 2 * i),
            _layer_spec(nw, layer),
            _layer_spec(wfox, layer, True), _layer_spec(whg, layer, True),
            _layer_spec(wff, layer, True),
        ],
        out_specs=tuple(slab_spec if len(s.shape) == 3 else row_spec(s.shape[1])
                        for s in out_shape),
        out_shape=out_shape,
        compiler_params=pltpu.CompilerParams(
            dimension_semantics=("arbitrary",), vmem_limit_bytes=VMEM_LIMIT),
        name="inproj",
    )(head, stream, stream, nw, wfox, whg, wff)


def _gates_kernel(ff_ref, fb_ref, e_ref, js_ref, *, padf):
    lp = ff_ref.shape[0]
    t = SEQ_TILE
    nblk = lp // t
    nrow = -(-nblk // 8) * 8
    h8 = FOX_HEADS
    row = lax.broadcasted_iota(jnp.int32, (t, t), 0)
    col = lax.broadcasted_iota(jnp.int32, (t, t), 1)
    tri = (row >= col).astype(BF16)
    lane = lax.broadcasted_iota(jnp.int32, (t, LANES), 1)
    rloc = lax.broadcasted_iota(jnp.int32, (t, LANES), 0)
    brow = lax.broadcasted_iota(jnp.int32, (nrow, LANES), 0)
    blane = lax.broadcasted_iota(jnp.int32, (nrow, LANES), 1)
    carry = jnp.zeros((1, LANES), F32)
    bmax = jnp.zeros((nrow, LANES), F32)
    bmin = jnp.zeros((nrow, LANES), F32)
    qnorm = jnp.zeros((nrow, LANES), F32)
    knorm = jnp.zeros((nrow, LANES), F32)
    for blk in range(nblk):
        raw = ff_ref[blk * t:(blk + 1) * t, :]
        z = raw + fb_ref[...]
        logf = jnp.minimum(z, 0.0) - jnp.log1p(jnp.exp(-jnp.abs(z)))
        hi, mid, lo = _split3(logf)
        c = (jnp.dot(tri, hi, preferred_element_type=F32)
             + jnp.dot(tri, mid, preferred_element_type=F32)
             + jnp.dot(tri, lo, preferred_element_type=F32)) + carry
        carry = c[t - 1:t, :]
        bias = jnp.where(rloc + blk * t >= padf, -LOG2E * c, MASK_VALUE)
        bh, bm, bl = _split3(bias)
        packed = jnp.where(
            lane < h8, bh.astype(F32),
            jnp.where(lane < 2 * h8, pltpu.roll(bm.astype(F32), h8, 1),
                      jnp.where(lane < 3 * h8, pltpu.roll(bl.astype(F32), 2 * h8, 1), 0.0)))
        e_ref[blk * t:(blk + 1) * t, :] = packed.astype(BF16)

        here = brow == blk
        sq = jnp.max(raw, axis=0, keepdims=True) * NORM_SLACK
        bmax = jnp.where(here, jnp.max(bias, axis=0, keepdims=True), bmax)
        bmin = jnp.where(here, jnp.min(bias, axis=0, keepdims=True), bmin)
        qnorm = jnp.where(here, jnp.sqrt(pltpu.roll(sq, LANES - h8, 1)), qnorm)
        knorm = jnp.where(here, jnp.sqrt(pltpu.roll(sq, LANES - 2 * h8, 1)), knorm)

    jlane = lax.broadcasted_iota(jnp.int32, js_ref.shape, 1)
    js = jnp.zeros(js_ref.shape, jnp.int32)
    rowf = brow[:, 0:1].astype(F32)
    for i in range(nblk):
        bound = qnorm[i:i + 1] * (knorm + knorm[i:i + 1]) + bmax - bmin[i:i + 1]
        needed = jnp.where((blane < h8) & (bound > -SKIP_LOG2), 1.0, 0.0)
        needed = jnp.max(needed, axis=1, keepdims=True)
        first = jnp.min(jnp.where((needed > 0.0) | (rowf >= i), rowf, float(nrow)),
                        axis=0, keepdims=True)
        js = jnp.where(jlane == i, first.astype(jnp.int32), js)
    js_ref[...] = js


def _gates(ff, fb, layer, batch, lp, padf):
    return pl.pallas_call(
        functools.partial(_gates_kernel, padf=padf),
        grid=(batch,),
        in_specs=[pl.BlockSpec((lp, LANES), lambda b: (b, 0)),
                  _layer_spec(fb, layer)],
        out_specs=(pl.BlockSpec((lp, LANES), lambda b: (b, 0)),
                   pl.BlockSpec((8, LANES), lambda b: (b, 0))),
        out_shape=(jax.ShapeDtypeStruct((batch * lp, LANES), BF16),
                   jax.ShapeDtypeStruct((batch * 8, LANES), jnp.int32)),
        compiler_params=pltpu.CompilerParams(
            dimension_semantics=("arbitrary",), vmem_limit_bytes=VMEM_LIMIT),
        name="fox_gates",
    )(ff, fb)


def _fox_kernel(js_ref, qt_ref, k_ref, vt_ref, e_ref, o_ref):
    i = pl.program_id(1)
    t = SEQ_TILE
    half = FOX_HEAD_DIM
    npair = FOX_HEADS // 2
    n_one = 16
    srow = lax.broadcasted_iota(jnp.int32, (LANES, t), 0)
    key_i = lax.broadcasted_iota(jnp.int32, (t, 2 * t), 0)
    qry_i = lax.broadcasted_iota(jnp.int32, (t, 2 * t), 1)
    causal = key_i <= jnp.where(qry_i >= t, qry_i - t, qry_i)
    lanes_of = lambda p: slice(p * LANES, (p + 1) * LANES)

    def stacked_qt(p):
        top = qt_ref[0, lanes_of(p), :]
        zero = jnp.zeros_like(top)
        sel_a = ((srow == 2 * p) | (srow == FOX_HEADS + 2 * p)
                 | (srow == 2 * FOX_HEADS + 2 * p)).astype(BF16)
        sel_b = ((srow == 2 * p + 1) | (srow == FOX_HEADS + 2 * p + 1)
                 | (srow == 2 * FOX_HEADS + 2 * p + 1)).astype(BF16)
        col_a = jnp.concatenate([jnp.where(srow < half, top, zero), sel_a], axis=0)
        col_b = jnp.concatenate([jnp.where(srow >= half, top, zero), sel_b], axis=0)
        return jnp.concatenate([col_a, col_b], axis=1)

    qqt = [stacked_qt(p) for p in range(npair)]

    def scores_of(j, nblk, p, masked):
        rows = pl.ds(pl.multiple_of(j * t, t), nblk * t)
        kaug = jnp.concatenate([k_ref[rows, lanes_of(p)], e_ref[rows, :]], axis=1)
        st = jnp.dot(kaug, qqt[p], preferred_element_type=F32)
        return jnp.where(causal, st, MASK_VALUE) if masked else st

    def softmax_of(m_old, st):
        m_new = jnp.maximum(m_old, jnp.max(st, axis=0, keepdims=True))
        return m_new, jnp.exp2(m_old - m_new), jnp.exp2(st - m_new).astype(BF16)

    def pv_of(j, nblk, p, pr):
        vt = jnp.concatenate([vt_ref[j + b, lanes_of(p), :] for b in range(nblk)], axis=1)
        vaug = jnp.concatenate([vt, jnp.ones((n_one, nblk * t), BF16)], axis=0)
        return jnp.dot(vaug, pr, preferred_element_type=F32)

    def step(j, nblk, carry, masked):
        scores = [scores_of(j, nblk, p, masked) for p in range(npair)]
        probs = [softmax_of(carry[p][0], scores[p]) for p in range(npair)]
        return tuple((probs[p][0], probs[p][1] * carry[p][1] + pv_of(j, nblk, p, probs[p][2]))
                     for p in range(npair))

    init = tuple((jnp.full((1, 2 * t), -jnp.inf, F32), jnp.zeros((LANES + n_one, 2 * t), F32))
                 for _ in range(npair))
    j0 = js_ref[pl.program_id(0) * 8, i]
    n_vis = i - j0
    carry = lax.fori_loop(0, n_vis // 2, lambda jj, c: step(j0 + 2 * jj, 2, c, False), init)
    carry = lax.fori_loop(0, n_vis % 2, lambda _, c: step(i - 1, 1, c, False), carry)
    fin = step(i, 1, carry, True)
    for p in range(npair):
        acc = fin[p][1]
        ot = acc[0:LANES] / acc[LANES:LANES + 1]
        pair_t = jnp.concatenate([ot[0:half, 0:t], ot[half:LANES, t:2 * t]], axis=0)
        o_ref[:, lanes_of(p)] = pair_t.T.astype(o_ref.dtype)


def _fox(js, qt, k, vt, e, batch, lp):
    nq = lp // SEQ_TILE
    grid_spec = pltpu.PrefetchScalarGridSpec(
        num_scalar_prefetch=1,
        grid=(batch, nq),
        in_specs=[
            pl.BlockSpec((1, FOX_WIDTH, SEQ_TILE), lambda b, i, js: (b * nq + i, 0, 0)),
            pl.BlockSpec((lp, FOX_WIDTH), lambda b, i, js: (b, 0)),
            pl.BlockSpec((nq, FOX_WIDTH, SEQ_TILE), lambda b, i, js: (b, 0, 0)),
            pl.BlockSpec((lp, LANES), lambda b, i, js: (b, 0)),
        ],
        out_specs=pl.BlockSpec((SEQ_TILE, FOX_WIDTH), lambda b, i, js: (b * nq + i, 0)),
    )
    return pl.pallas_call(
        _fox_kernel,
        grid_spec=grid_spec,
        out_shape=jax.ShapeDtypeStruct((batch * lp, FOX_WIDTH), BF16),
        compiler_params=pltpu.CompilerParams(
            dimension_semantics=("arbitrary", "arbitrary"), vmem_limit_bytes=VMEM_LIMIT),
        name="fox_attention",
    )(js, qt, k, vt, e)


def _hgrn_constants():
    c = CHUNK
    m = np.zeros((N_EXP_ROWS, c), np.float32)
    upper = np.zeros((N_LEVELS, c), np.float32)
    for t in range(c):
        m[t, :t + 1] = 1.0
        m[c + t, t + 1:] = 1.0
        for lv in range(N_LEVELS):
            w = c >> (lv + 1)
            mid = (t // (2 * w)) * 2 * w + w
            if t >= mid:
                m[(2 + lv) * c + t, mid:t + 1] = 1.0
                upper[lv, t] = 1.0
            else:
                m[(2 + lv) * c + t, t + 1:mid] = 1.0
    level = np.full((c, c), N_LEVELS + 1, np.int32)
    for t in range(c):
        level[t, t] = N_LEVELS
        for s in range(t):
            w = 1 << int(np.floor(np.log2(t ^ s)))
            level[t, s] = int(np.log2(c // (2 * w)))
    m3 = np.concatenate([m, m, m], axis=1)
    return m3, upper, level


def _hgrn_kernel(hq_ref, hf_ref, hi_ref, hg_ref, lbraw_ref, nw_ref, m3_ref, up_ref, lvl_ref,
                 o_ref, st_ref, *, layer):
    @pl.when(pl.program_id(1) == 0)
    def _():
        st_ref[...] = jnp.zeros_like(st_ref)

    c = CHUNK
    raw = lbraw_ref[...]
    ex = jnp.exp(raw - jnp.max(raw, axis=0, keepdims=True))
    s_lb = ex / jnp.sum(ex, axis=0, keepdims=True)
    lb = jnp.sum(s_lb[:layer + 1], axis=0, keepdims=True) - s_lb[0:1]
    level = lvl_ref[...]
    nt = (((1,), (1,)), ((), ()))
    tn = (((0,), (0,)), ((), ()))
    n_chunks = hq_ref.shape[0] // c
    heads = [slice(h * HG_K, (h + 1) * HG_K) for h in range(HG_HEADS)]

    one_m_lb = 1.0 - lb
    floor2 = LOG_F_MIN * LOG2E
    level_is = [level == lv for lv in range(N_LEVELS + 1)]
    gates, keys, queries = [], [], []
    for ci in range(n_chunks):
        z = hf_ref[ci * c:(ci + 1) * c, :]
        omf = one_m_lb / (1.0 + jnp.exp2(LOG2E * z))
        lg2 = jnp.log2(1.0 - omf)
        gates.append(jnp.maximum(lg2, floor2))
        keys.append(jnp.where(lg2 >= floor2, omf, 1.0 - np.exp(LOG_F_MIN)))
        queries.append(_silu(hq_ref[ci * c:(ci + 1) * c, :], HG_K ** -0.5))
    exps = [jnp.dot(m3_ref[...], jnp.concatenate(_split3(g), axis=0),
                    preferred_element_type=F32) for g in gates]

    o_intra, q_dec, k_dec, dec_last = [], [], [], []
    for ci in range(n_chunks):
        q, kk, e_all = queries[ci], keys[ci], exps[ci]
        v = hi_ref[ci * c:(ci + 1) * c, :]
        b = e_all[0:c]
        q_dec.append((q * jnp.exp2(b)).astype(BF16))
        k_dec.append((kk * jnp.exp2(e_all[c:2 * c])).astype(BF16))
        dec_last.append(jnp.exp2(b[c - 1:c, :]))
        rl = [(jnp.where(up_ref[lv] > 0.5, q, kk)
               * jnp.exp2(e_all[(2 + lv) * c:(3 + lv) * c])).astype(BF16)
              for lv in range(N_LEVELS)]
        qb, kb = q.astype(BF16), kk.astype(BF16)
        per_head = []
        for hs in heads:
            a = jnp.where(level_is[N_LEVELS],
                          lax.dot_general(qb[:, hs], kb[:, hs], nt, preferred_element_type=F32),
                          0.0)
            for lv in range(N_LEVELS):
                r = rl[lv][:, hs]
                a = jnp.where(level_is[lv],
                              lax.dot_general(r, r, nt, preferred_element_type=F32), a)
            per_head.append(jnp.dot(a.astype(BF16), v[:, hs], preferred_element_type=F32))
        o_intra.append(per_head)

    states = [st_ref[h] for h in range(HG_HEADS)]
    for ci in range(n_chunks):
        rows = slice(ci * c, (ci + 1) * c)
        v = hi_ref[rows, :]
        gate = _silu(hg_ref[rows, :])
        for h, hs in enumerate(heads):
            st = states[h]
            o = o_intra[ci][h] + lax.dot_general(q_dec[ci][:, hs], st.astype(BF16), nt,
                                                 preferred_element_type=F32)
            states[h] = st * dec_last[ci][:, hs] + lax.dot_general(
                v[:, hs], k_dec[ci][:, hs], tn, preferred_element_type=F32)
            o_ref[rows, hs] = (_rms(o, nw_ref[...]) * gate[:, hs]).astype(o_ref.dtype)
    for h in range(HG_HEADS):
        st_ref[h] = states[h]


def _hgrn(hq, hf, hi, hg, lb_raw, nw, layer, batch, lp):
    nb = lp // SEQ_TILE
    m3, upper, level = _hgrn_constants()
    row_spec = pl.BlockSpec((SEQ_TILE, HG_WIDTH), lambda b, i: (b * nb + i, 0))
    full = lambda shape: pl.BlockSpec(shape, lambda b, i: (0,) * len(shape))
    depth = lb_raw.shape[0]
    return pl.pallas_call(
        functools.partial(_hgrn_kernel, layer=layer),
        grid=(batch, nb),
        in_specs=[row_spec, row_spec, row_spec, row_spec,
                  full((depth, HG_WIDTH)), _layer_spec(nw, layer),
                  full(m3.shape), full((N_LEVELS, CHUNK, 1)), full(level.shape)],
        out_specs=row_spec,
        out_shape=jax.ShapeDtypeStruct((batch * lp, HG_WIDTH), BF16),
        scratch_shapes=[pltpu.VMEM((HG_HEADS, HG_V, HG_K), F32)],
        compiler_params=pltpu.CompilerParams(
            dimension_semantics=("arbitrary", "arbitrary"), vmem_limit_bytes=VMEM_LIMIT),
        name="hgrn2",
    )(hq, hf, hi, hg, lb_raw, nw, jnp.asarray(m3, BF16),
      jnp.asarray(upper[:, :, None], F32), jnp.asarray(level))


def _mlp_kernel(head_ref, lo_ref, hi_ref, fo_lo_ref, fo_hi_ref, ho_lo_ref, ho_hi_ref, wo_ref,
                nw_ref, wg_ref, wu_ref, wd_ref, fnw_ref, out_ref, a_ref, *, first, final, g):
    if final:
        hh = pl.program_id(0) * g.nb + 1 + 2 * pl.program_id(1)
    else:
        hh = 2 * pl.program_id(0)
    h = _stream_rows(head_ref, lo_ref, hi_ref, hh, first, g)
    fo = jnp.concatenate([fo_lo_ref[0], fo_hi_ref[0]], axis=0)
    ho = jnp.concatenate([ho_lo_ref[0], ho_hi_ref[0]], axis=0)
    mixed = (jnp.dot(fo, wo_ref[0:FOX_WIDTH, :], preferred_element_type=F32)
             + jnp.dot(ho, wo_ref[FOX_WIDTH:, :], preferred_element_type=F32))
    h1 = h + mixed
    u = _rms(h1, nw_ref[...])
    dff = wg_ref.shape[1]
    for c0 in range(0, dff, FF_TILE):
        cs = slice(c0, c0 + FF_TILE)
        gate = jnp.dot(u, wg_ref[:, cs], preferred_element_type=F32)
        up = jnp.dot(u, wu_ref[:, cs], preferred_element_type=F32)
        a_ref[:, cs] = (_silu(gate) * up).astype(BF16)
    h2 = h1 + jnp.dot(a_ref[...], wd_ref[...], preferred_element_type=F32)
    if final:
        h2 = _rms(h2, fnw_ref[...])
    else:
        row = lax.broadcasted_iota(jnp.int32, (ROW_TILE, 1), 0)
        is_pad = (((row < g.pad) & (hh % g.nb == 0))
                  | ((row >= SEQ_TILE) & (row < SEQ_TILE + g.pad) & ((hh + 1) % g.nb == 0)))
        h2 = jnp.where(is_pad, 0.0, h2)
    out_ref[...] = h2


def _mlp(head, stream, first, g, layer, fo, ho, wo, nw, wg, wu, wd, fnw, final):
    d = head.shape[1]
    dff = wg.shape[-1]
    if final:
        grid = (g.batch, g.sb // 2)
        block_of = lambda b, k: b * g.nb + 1 + 2 * k
        out_rows = g.batch * g.sb * SEQ_TILE
        out_spec = pl.BlockSpec((ROW_TILE, d), lambda b, k: (b * (g.sb // 2) + k, 0))
    else:
        grid = (g.batch * g.nb // 2,)
        block_of = lambda i: 2 * i
        out_rows = g.batch * g.nb * SEQ_TILE
        out_spec = pl.BlockSpec((ROW_TILE, d), lambda i: (i, 0))
    const = lambda a: _layer_spec(a, layer, True)
    small = lambda shape: pl.BlockSpec(shape, lambda *_: (0, 0))
    blocks = lambda a: a.reshape(-1, SEQ_TILE, a.shape[-1])
    fo3, ho3 = blocks(fo), blocks(ho)
    return pl.pallas_call(
        functools.partial(_mlp_kernel, first=first, final=final, g=g),
        grid=grid,
        in_specs=[small(head.shape),
                  *_stream_specs(first, g, d, block_of),
                  *_stream_specs(False, g, FOX_WIDTH, block_of),
                  *_stream_specs(False, g, HG_WIDTH, block_of),
                  const(wo), _layer_spec(nw, layer), const(wg), const(wu), const(wd),
                  small((1, d))],
        out_specs=out_spec,
        out_shape=jax.ShapeDtypeStruct((out_rows, d), F32),
        scratch_shapes=[pltpu.VMEM((ROW_TILE, dff), BF16)],
        compiler_params=pltpu.CompilerParams(
            dimension_semantics=("arbitrary",) * len(grid), vmem_limit_bytes=VMEM_LIMIT),
        name="mlp_final" if final else "mlp",
    )(head, stream, stream, fo3, fo3, ho3, ho3, wo, nw, wg, wu, wd, fnw)


def kernel(x, meta, norm_mix_w, w_in, fox_f_bias, hgrn_lb_raw, hgrn_norm_w, w_out,
           norm_ffn_w, w_ffn_gate, w_ffn_up, w_ffn_down, norm_final_w):
    batch, seq, d = x.shape
    depth = w_in.shape[0]
    assert seq % ROW_TILE == 0 and N_META <= SEQ_TILE
    lp = seq + SEQ_TILE
    assert (batch * lp) % ROW_TILE == 0
    padf = SEQ_TILE - N_META
    g = _Geom(batch=batch, nb=lp // SEQ_TILE, sb=seq // SEQ_TILE, pad=padf)

    head = jnp.concatenate([jnp.zeros((padf, d), x.dtype), meta.astype(x.dtype)], axis=0)
    stream = x.reshape(batch * g.sb, SEQ_TILE, d)

    wfox, whg, wff = _split_w_in(w_in.astype(F32))
    wo, wg, wu, wd = (w.astype(F32) for w in (w_out, w_ffn_gate, w_ffn_up, w_ffn_down))
    fb = jnp.pad(fox_f_bias.astype(F32), ((0, 0), (0, LANES - FOX_HEADS)))[:, None]
    lb_raw = hgrn_lb_raw.astype(F32)
    nw_mix, nw_ffn, nw_hg = (a.astype(F32)[:, None]
                             for a in (norm_mix_w, norm_ffn_w, hgrn_norm_w))
    for l in range(depth):
        first, final = l == 0, l == depth - 1
        qt, k, vt, hq, hf, hi, hg, ff = _inproj(head, stream, first, g, l, nw_mix, wfox, whg, wff)
        e, js = _gates(ff, fb, l, batch, lp, padf)
        fox_o = _fox(js, qt, k, vt, e, batch, lp)
        hgrn_o = _hgrn(hq, hf, hi, hg, lb_raw, nw_hg, l, batch, lp)
        out = _mlp(head, stream, first, g, l, fox_o, hgrn_o, wo, nw_ffn, wg, wu, wd,
                   norm_final_w[None].astype(F32), final)
        stream = out.reshape(-1, SEQ_TILE, d)
    return out.reshape(batch, seq, d)
```

```python
import functools
from typing import NamedTuple

import numpy as np
import jax
import jax.numpy as jnp
from jax import lax
from jax.experimental import pallas as pl
from jax.experimental.pallas import tpu as pltpu

F32 = jnp.float32
BF16 = jnp.bfloat16

N_META = 16
FOX_HEADS = 8
FOX_HEAD_DIM = 64
FOX_WIDTH = FOX_HEADS * FOX_HEAD_DIM
HG_HEADS = 4
HG_K = 128
HG_V = 128
HG_WIDTH = HG_HEADS * HG_K
CHUNK = 64
EPS = 1e-6
MASK_VALUE = -1e30
LOG_F_MIN = -30.0
LOG2E = 1.4426950408889634
SKIP_LOG2 = 160.0
NORM_SLACK = 1.02

LANES = 128
SEQ_TILE = 256
ROW_TILE = 512
FF_TILE = 256
VMEM_LIMIT = 56 * 1024 * 1024

N_LEVELS = 6
N_EXP_ROWS = (2 + N_LEVELS) * CHUNK


def _rms(x, w):
    ms = jnp.mean(x * x, axis=-1, keepdims=True)
    return x * lax.rsqrt(ms + EPS) * w


def _split3(x):
    hi = x.astype(BF16)
    r1 = x - hi.astype(F32)
    mid = r1.astype(BF16)
    lo = (r1 - mid.astype(F32)).astype(BF16)
    return hi, mid, lo


def _silu(x, scale=1.0):
    return (scale * x) / (1.0 + jnp.exp2(-LOG2E * x))


def _split_w_in_kernel(wt_ref, fox_ref, hg_ref, ff_ref):
    fw = 3 * FOX_WIDTH
    t = SEQ_TILE

    def block(r0, rows=t):
        return wt_ref[r0:r0 + rows, :]

    for j in range(fw // t):
        fox_ref[:, j * t:(j + 1) * t] = block(j * t).T.astype(BF16)
    for j in range(hg_ref.shape[1] // t):
        hg_ref[:, j * t:(j + 1) * t] = block(fw + FOX_HEADS + j * t).T.astype(BF16)
    ffb = block(fw, LANES)
    row = lax.broadcasted_iota(jnp.int32, ffb.shape, 0)
    ff_ref[...] = jnp.where(row < FOX_HEADS, ffb, 0.0).T.astype(BF16)


def _split_w_in(w):
    depth, kdim, n = w.shape
    fw = 3 * FOX_WIDTH
    widths = (fw, n - fw - FOX_HEADS, LANES)
    wt = jnp.transpose(w, (0, 2, 1))
    return pl.pallas_call(
        _split_w_in_kernel,
        grid=(depth,),
        in_specs=[pl.BlockSpec((None, n, kdim), lambda l: (l, 0, 0))],
        out_specs=tuple(pl.BlockSpec((None, kdim, c), lambda l: (l, 0, 0)) for c in widths),
        out_shape=tuple(jax.ShapeDtypeStruct((depth, kdim, c), BF16) for c in widths),
        compiler_params=pltpu.CompilerParams(
            dimension_semantics=("arbitrary",), vmem_limit_bytes=VMEM_LIMIT),
        name="split_w_in",
    )(wt)


def _layer_spec(a, layer, resident=False):
    tail = a.shape[1:]
    index = lambda *_: (layer,) + (0,) * len(tail)
    if resident:
        return pl.BlockSpec((None,) + tail, index, pipeline_mode=pl.Buffered(1))
    return pl.BlockSpec((None,) + tail, index)


class _Geom(NamedTuple):
    batch: int
    nb: int
    sb: int
    pad: int


def _stream_specs(first, g, d, block_of):
    def spec(half):
        def index(*idx):
            hh = block_of(*idx) + half
            if first:
                hh = (hh // g.nb) * g.sb + jnp.maximum(hh % g.nb - 1, 0)
            return (hh, 0, 0)
        return pl.BlockSpec((1, SEQ_TILE, d), index)
    return [spec(0), spec(1)]


def _stream_rows(head_ref, lo_ref, hi_ref, hh, first, g):
    lo, hi = lo_ref[0], hi_ref[0]
    if first:
        lo = jnp.where(hh % g.nb == 0, head_ref[...], lo)
        hi = jnp.where((hh + 1) % g.nb == 0, head_ref[...], hi)
    return jnp.concatenate([lo, hi], axis=0)


def _head_norm_max(xt):
    sq = xt * xt
    lane = lax.broadcasted_iota(jnp.int32, (1, LANES), 1)
    out = jnp.zeros((1, LANES), F32)
    for h in range(FOX_HEADS):
        n2 = jnp.sum(sq[h * FOX_HEAD_DIM:(h + 1) * FOX_HEAD_DIM], axis=0, keepdims=True)
        out = jnp.where(lane == h, jnp.max(n2, axis=1, keepdims=True), out)
    return out


def _inproj_kernel(head_ref, lo_ref, hi_ref, nw_ref, wfox_ref, whg_ref, wff_ref, qt_ref, k_ref,
                   vt_ref, hq_ref, hf_ref, hi_out_ref, hg_ref, ff_ref, nrm_ref, *, first, g):
    h = _stream_rows(head_ref, lo_ref, hi_ref, 2 * pl.program_id(0), first, g)
    u = _rms(h, nw_ref[...]).astype(BF16)

    def mm(c0, c1, w_ref=wfox_ref):
        return jnp.dot(u, w_ref[:, c0:c1], preferred_element_type=F32)

    w = FOX_WIDTH
    q = mm(0, w) * (FOX_HEAD_DIM ** -0.5 * LOG2E)
    kf = mm(w, 2 * w)
    v = mm(2 * w, 3 * w)
    k_ref[...] = kf.astype(BF16)
    for s in range(ROW_TILE // SEQ_TILE):
        rows = slice(s * SEQ_TILE, (s + 1) * SEQ_TILE)
        qts = q[rows].T
        qt_ref[s] = qts.astype(BF16)
        vt_ref[s] = v[rows].T.astype(BF16)
        nrm_ref[s] = jnp.concatenate(
            [_head_norm_max(qts), _head_norm_max(kf[rows].T), jnp.zeros((6, LANES), F32)], axis=0)
    hq_ref[...] = mm(0, HG_WIDTH, whg_ref).astype(BF16)
    hf_ref[...] = mm(HG_WIDTH, 2 * HG_WIDTH, whg_ref)
    hi_out_ref[...] = mm(2 * HG_WIDTH, 3 * HG_WIDTH, whg_ref).astype(BF16)
    hg_ref[...] = mm(3 * HG_WIDTH, 4 * HG_WIDTH, whg_ref).astype(BF16)
    ff_ref[...] = mm(0, LANES, wff_ref)


def _inproj(head, stream, first, g, layer, nw, wfox, whg, wff):
    d = head.shape[1]
    rows = g.batch * g.nb * SEQ_TILE
    grid = (rows // ROW_TILE,)
    row_spec = lambda c: pl.BlockSpec((ROW_TILE, c), lambda i: (i, 0))
    slabs = ROW_TILE // SEQ_TILE
    slab_shape = jax.ShapeDtypeStruct((rows // SEQ_TILE, FOX_WIDTH, SEQ_TILE), BF16)
    slab_spec = pl.BlockSpec((slabs, FOX_WIDTH, SEQ_TILE), lambda i: (i, 0, 0))
    out_shape = (
        slab_shape,
        jax.ShapeDtypeStruct((rows, FOX_WIDTH), BF16),
        slab_shape,
        jax.ShapeDtypeStruct((rows, HG_WIDTH), BF16),
        jax.ShapeDtypeStruct((rows, HG_WIDTH), F32),
        jax.ShapeDtypeStruct((rows, HG_WIDTH), BF16),
        jax.ShapeDtypeStruct((rows, HG_WIDTH), BF16),
        jax.ShapeDtypeStruct((rows, LANES), F32),
        jax.ShapeDtypeStruct((rows // SEQ_TILE, 8, LANES), F32),
    )
    norm_spec = pl.BlockSpec((slabs, 8, LANES), lambda i: (i, 0, 0))
    return pl.pallas_call(
        functools.partial(_inproj_kernel, first=first, g=g),
        grid=grid,
        in_specs=[
            pl.BlockSpec(head.shape, lambda i: (0, 0)),
            *_stream_specs(first, g, d, lambda i: 2 * i),
            _layer_spec(nw, layer),
            _layer_spec(wfox, layer, True), _layer_spec(whg, layer, True),
            _layer_spec(wff, layer, True),
        ],
        out_specs=tuple(row_spec(s.shape[1]) if len(s.shape) == 2
                        else (norm_spec if s.dtype == F32 else slab_spec) for s in out_shape),
        out_shape=out_shape,
        compiler_params=pltpu.CompilerParams(
            dimension_semantics=("arbitrary",), vmem_limit_bytes=VMEM_LIMIT),
        name="inproj",
    )(head, stream, stream, nw, wfox, whg, wff)


def _gates_kernel(ff_ref, fb_ref, nrm_ref, e_ref, js_ref, *, padf):
    lp = ff_ref.shape[0]
    t = SEQ_TILE
    nblk = lp // t
    nrow = -(-nblk // 8) * 8
    h8 = FOX_HEADS
    row = lax.broadcasted_iota(jnp.int32, (t, t), 0)
    col = lax.broadcasted_iota(jnp.int32, (t, t), 1)
    tri = (row >= col).astype(BF16)
    lane = lax.broadcasted_iota(jnp.int32, (t, LANES), 1)
    rloc = lax.broadcasted_iota(jnp.int32, (t, LANES), 0)
    brow = lax.broadcasted_iota(jnp.int32, (nrow, LANES), 0)
    blane = lax.broadcasted_iota(jnp.int32, (nrow, LANES), 1)
    carry = jnp.zeros((1, LANES), F32)
    bmax = jnp.zeros((nrow, LANES), F32)
    bmin = jnp.zeros((nrow, LANES), F32)
    qnorm = jnp.zeros((nrow, LANES), F32)
    knorm = jnp.zeros((nrow, LANES), F32)
    for blk in range(nblk):
        raw = ff_ref[blk * t:(blk + 1) * t, :]
        z = raw + fb_ref[...]
        logf = jnp.minimum(z, 0.0) - jnp.log1p(jnp.exp(-jnp.abs(z)))
        hi, mid, lo = _split3(logf)
        c = (jnp.dot(tri, hi, preferred_element_type=F32)
             + jnp.dot(tri, mid, preferred_element_type=F32)
             + jnp.dot(tri, lo, preferred_element_type=F32)) + carry
        carry = c[t - 1:t, :]
        bias = jnp.where(rloc + blk * t >= padf, -LOG2E * c, MASK_VALUE)
        bh, bm, bl = _split3(bias)
        packed = jnp.where(
            lane < h8, bh.astype(F32),
            jnp.where(lane < 2 * h8, pltpu.roll(bm.astype(F32), h8, 1),
                      jnp.where(lane < 3 * h8, pltpu.roll(bl.astype(F32), 2 * h8, 1), 0.0)))
        e_ref[blk * t:(blk + 1) * t, :] = packed.astype(BF16)

        here = brow == blk
        bmax = jnp.where(here, jnp.max(bias, axis=0, keepdims=True), bmax)
        bmin = jnp.where(here, jnp.min(bias, axis=0, keepdims=True), bmin)
        qnorm = jnp.where(here, jnp.sqrt(nrm_ref[blk, 0:1, :] * NORM_SLACK), qnorm)
        knorm = jnp.where(here, jnp.sqrt(nrm_ref[blk, 1:2, :] * NORM_SLACK), knorm)

    jlane = lax.broadcasted_iota(jnp.int32, js_ref.shape, 1)
    js = jnp.zeros(js_ref.shape, jnp.int32)
    rowf = brow[:, 0:1].astype(F32)
    for i in range(nblk):
        bound = qnorm[i:i + 1] * (knorm + knorm[i:i + 1]) + bmax - bmin[i:i + 1]
        needed = jnp.where((blane < h8) & (bound > -SKIP_LOG2), 1.0, 0.0)
        needed = jnp.max(needed, axis=1, keepdims=True)
        first = jnp.min(jnp.where((needed > 0.0) | (rowf >= i), rowf, float(nrow)),
                        axis=0, keepdims=True)
        js = jnp.where(jlane == i, first.astype(jnp.int32), js)
    js_ref[...] = js


def _gates(ff, fb, nrm, layer, batch, lp, padf):
    return pl.pallas_call(
        functools.partial(_gates_kernel, padf=padf),
        grid=(batch,),
        in_specs=[pl.BlockSpec((lp, LANES), lambda b: (b, 0)),
                  _layer_spec(fb, layer),
                  pl.BlockSpec((lp // SEQ_TILE, 8, LANES), lambda b: (b, 0, 0))],
        out_specs=(pl.BlockSpec((lp, LANES), lambda b: (b, 0)),
                   pl.BlockSpec((8, LANES), lambda b: (b, 0))),
        out_shape=(jax.ShapeDtypeStruct((batch * lp, LANES), BF16),
                   jax.ShapeDtypeStruct((batch * 8, LANES), jnp.int32)),
        compiler_params=pltpu.CompilerParams(
            dimension_semantics=("arbitrary",), vmem_limit_bytes=VMEM_LIMIT),
        name="fox_gates",
    )(ff, fb, nrm)


def _fox_kernel(js_ref, qt_ref, k_ref, vt_ref, e_ref, o_ref):
    i = pl.program_id(1)
    t = SEQ_TILE
    half = FOX_HEAD_DIM
    npair = FOX_HEADS // 2
    n_one = 16
    srow = lax.broadcasted_iota(jnp.int32, (LANES, t), 0)
    key_i = lax.broadcasted_iota(jnp.int32, (t, 2 * t), 0)
    qry_i = lax.broadcasted_iota(jnp.int32, (t, 2 * t), 1)
    causal = key_i <= jnp.where(qry_i >= t, qry_i - t, qry_i)
    lanes_of = lambda p: slice(p * LANES, (p + 1) * LANES)

    def stacked_qt(p):
        top = qt_ref[0, lanes_of(p), :]
        zero = jnp.zeros_like(top)
        sel_a = ((srow == 2 * p) | (srow == FOX_HEADS + 2 * p)
                 | (srow == 2 * FOX_HEADS + 2 * p)).astype(BF16)
        sel_b = ((srow == 2 * p + 1) | (srow == FOX_HEADS + 2 * p + 1)
                 | (srow == 2 * FOX_HEADS + 2 * p + 1)).astype(BF16)
        col_a = jnp.concatenate([jnp.where(srow < half, top, zero), sel_a], axis=0)
        col_b = jnp.concatenate([jnp.where(srow >= half, top, zero), sel_b], axis=0)
        return jnp.concatenate([col_a, col_b], axis=1)

    qqt = [stacked_qt(p) for p in range(npair)]

    def scores_of(j, nblk, p, masked):
        rows = pl.ds(pl.multiple_of(j * t, t), nblk * t)
        kaug = jnp.concatenate([k_ref[rows, lanes_of(p)], e_ref[rows, :]], axis=1)
        st = jnp.dot(kaug, qqt[p], preferred_element_type=F32)
        return jnp.where(causal, st, MASK_VALUE) if masked else st

    def softmax_of(m_old, st):
        m_new = jnp.maximum(m_old, jnp.max(st, axis=0, keepdims=True))
        return m_new, jnp.exp2(m_old - m_new), jnp.exp2(st - m_new).astype(BF16)

    def pv_of(j, nblk, p, pr):
        vt = jnp.concatenate([vt_ref[j + b, lanes_of(p), :] for b in range(nblk)], axis=1)
        vaug = jnp.concatenate([vt, jnp.ones((n_one, nblk * t), BF16)], axis=0)
        return jnp.dot(vaug, pr, preferred_element_type=F32)

    def step(j, nblk, carry, masked):
        scores = [scores_of(j, nblk, p, masked) for p in range(npair)]
        probs = [softmax_of(carry[p][0], scores[p]) for p in range(npair)]
        return tuple((probs[p][0], probs[p][1] * carry[p][1] + pv_of(j, nblk, p, probs[p][2]))
                     for p in range(npair))

    init = tuple((jnp.full((1, 2 * t), -jnp.inf, F32), jnp.zeros((LANES + n_one, 2 * t), F32))
                 for _ in range(npair))
    j0 = js_ref[pl.program_id(0) * 8, i]
    n_vis = i - j0
    carry = lax.fori_loop(0, n_vis // 2, lambda jj, c: step(j0 + 2 * jj, 2, c, False), init)
    carry = lax.fori_loop(0, n_vis % 2, lambda _, c: step(i - 1, 1, c, False), carry)
    fin = step(i, 1, carry, True)
    for p in range(npair):
        acc = fin[p][1]
        ot = acc[0:LANES] / acc[LANES:LANES + 1]
        pair_t = jnp.concatenate([ot[0:half, 0:t], ot[half:LANES, t:2 * t]], axis=0)
        o_ref[:, lanes_of(p)] = pair_t.T.astype(o_ref.dtype)


def _fox(js, qt, k, vt, e, batch, lp):
    nq = lp // SEQ_TILE
    grid_spec = pltpu.PrefetchScalarGridSpec(
        num_scalar_prefetch=1,
        grid=(batch, nq),
        in_specs=[
            pl.BlockSpec((1, FOX_WIDTH, SEQ_TILE), lambda b, i, js: (b * nq + i, 0, 0)),
            pl.BlockSpec((lp, FOX_WIDTH), lambda b, i, js: (b, 0)),
            pl.BlockSpec((nq, FOX_WIDTH, SEQ_TILE), lambda b, i, js: (b, 0, 0)),
            pl.BlockSpec((lp, LANES), lambda b, i, js: (b, 0)),
        ],
        out_specs=pl.BlockSpec((SEQ_TILE, FOX_WIDTH), lambda b, i, js: (b * nq + i, 0)),
    )
    return pl.pallas_call(
        _fox_kernel,
        grid_spec=grid_spec,
        out_shape=jax.ShapeDtypeStruct((batch * lp, FOX_WIDTH), BF16),
        compiler_params=pltpu.CompilerParams(
            dimension_semantics=("arbitrary", "arbitrary"), vmem_limit_bytes=VMEM_LIMIT),
        name="fox_attention",
    )(js, qt, k, vt, e)


def _hgrn_constants():
    c = CHUNK
    m = np.zeros((N_EXP_ROWS, c), np.float32)
    upper = np.zeros((N_LEVELS, c), np.float32)
    for t in range(c):
        m[t, :t + 1] = 1.0
        m[c + t, t + 1:] = 1.0
        for lv in range(N_LEVELS):
            w = c >> (lv + 1)
            mid = (t // (2 * w)) * 2 * w + w
            if t >= mid:
                m[(2 + lv) * c + t, mid:t + 1] = 1.0
                upper[lv, t] = 1.0
            else:
                m[(2 + lv) * c + t, t + 1:mid] = 1.0
    level = np.full((c, c), N_LEVELS + 1, np.int32)
    for t in range(c):
        level[t, t] = N_LEVELS
        for s in range(t):
            w = 1 << int(np.floor(np.log2(t ^ s)))
            level[t, s] = int(np.log2(c // (2 * w)))
    m3 = np.concatenate([m, m, m], axis=1)
    return m3, upper, level


def _hgrn_kernel(hq_ref, hf_ref, hi_ref, hg_ref, lbraw_ref, nw_ref, m3_ref, up_ref, lvl_ref,
                 o_ref, st_ref, *, layer):
    @pl.when(pl.program_id(1) == 0)
    def _():
        st_ref[...] = jnp.zeros_like(st_ref)

    c = CHUNK
    raw = lbraw_ref[...]
    ex = jnp.exp(raw - jnp.max(raw, axis=0, keepdims=True))
    s_lb = ex / jnp.sum(ex, axis=0, keepdims=True)
    lb = jnp.sum(s_lb[:layer + 1], axis=0, keepdims=True) - s_lb[0:1]
    level = lvl_ref[...]
    nt = (((1,), (1,)), ((), ()))
    tn = (((0,), (0,)), ((), ()))
    n_chunks = hq_ref.shape[0] // c
    heads = [slice(h * HG_K, (h + 1) * HG_K) for h in range(HG_HEADS)]

    one_m_lb = 1.0 - lb
    floor2 = LOG_F_MIN * LOG2E
    level_is = [level == lv for lv in range(N_LEVELS + 1)]
    gates, keys, queries = [], [], []
    for ci in range(n_chunks):
        z = hf_ref[ci * c:(ci + 1) * c, :]
        omf = one_m_lb / (1.0 + jnp.exp2(LOG2E * z))
        lg2 = jnp.log2(1.0 - omf)
        gates.append(jnp.maximum(lg2, floor2))
        keys.append(jnp.where(lg2 >= floor2, omf, 1.0 - np.exp(LOG_F_MIN)))
        queries.append(_silu(hq_ref[ci * c:(ci + 1) * c, :].astype(F32), HG_K ** -0.5))
    exps = [jnp.dot(m3_ref[...], jnp.concatenate(_split3(g), axis=0),
                    preferred_element_type=F32) for g in gates]

    o_intra, q_dec, k_dec, dec_last = [], [], [], []
    for ci in range(n_chunks):
        q, kk, e_all = queries[ci], keys[ci], exps[ci]
        v = hi_ref[ci * c:(ci + 1) * c, :]
        b = e_all[0:c]
        q_dec.append((q * jnp.exp2(b)).astype(BF16))
        k_dec.append((kk * jnp.exp2(e_all[c:2 * c])).astype(BF16))
        dec_last.append(jnp.exp2(b[c - 1:c, :]))
        rl = [(jnp.where(up_ref[lv] > 0.5, q, kk)
               * jnp.exp2(e_all[(2 + lv) * c:(3 + lv) * c])).astype(BF16)
              for lv in range(N_LEVELS)]
        qb, kb = q.astype(BF16), kk.astype(BF16)
        per_head = []
        for hs in heads:
            a = jnp.where(level_is[N_LEVELS],
                          lax.dot_general(qb[:, hs], kb[:, hs], nt, preferred_element_type=F32),
                          0.0)
            for lv in range(N_LEVELS):
                r = rl[lv][:, hs]
                a = jnp.where(level_is[lv],
                              lax.dot_general(r, r, nt, preferred_element_type=F32), a)
            per_head.append(jnp.dot(a.astype(BF16), v[:, hs], preferred_element_type=F32))
        o_intra.append(per_head)

    states = [st_ref[h] for h in range(HG_HEADS)]
    for ci in range(n_chunks):
        rows = slice(ci * c, (ci + 1) * c)
        v = hi_ref[rows, :]
        gate = _silu(hg_ref[rows, :].astype(F32))
        for h, hs in enumerate(heads):
            st = states[h]
            o = o_intra[ci][h] + lax.dot_general(q_dec[ci][:, hs], st.astype(BF16), nt,
                                                 preferred_element_type=F32)
            states[h] = st * dec_last[ci][:, hs] + lax.dot_general(
                v[:, hs], k_dec[ci][:, hs], tn, preferred_element_type=F32)
            o_ref[rows, hs] = (_rms(o, nw_ref[...]) * gate[:, hs]).astype(o_ref.dtype)
    for h in range(HG_HEADS):
        st_ref[h] = states[h]


def _hgrn(hq, hf, hi, hg, lb_raw, nw, layer, batch, lp):
    nb = lp // SEQ_TILE
    m3, upper, level = _hgrn_constants()
    row_spec = pl.BlockSpec((SEQ_TILE, HG_WIDTH), lambda b, i: (b * nb + i, 0))
    full = lambda shape: pl.BlockSpec(shape, lambda b, i: (0,) * len(shape))
    depth = lb_raw.shape[0]
    return pl.pallas_call(
        functools.partial(_hgrn_kernel, layer=layer),
        grid=(batch, nb),
        in_specs=[row_spec, row_spec, row_spec, row_spec,
                  full((depth, HG_WIDTH)), _layer_spec(nw, layer),
                  full(m3.shape), full((N_LEVELS, CHUNK, 1)), full(level.shape)],
        out_specs=row_spec,
        out_shape=jax.ShapeDtypeStruct((batch * lp, HG_WIDTH), BF16),
        scratch_shapes=[pltpu.VMEM((HG_HEADS, HG_V, HG_K), F32)],
        compiler_params=pltpu.CompilerParams(
            dimension_semantics=("arbitrary", "arbitrary"), vmem_limit_bytes=VMEM_LIMIT),
        name="hgrn2",
    )(hq, hf, hi, hg, lb_raw, nw, jnp.asarray(m3, BF16),
      jnp.asarray(upper[:, :, None], F32), jnp.asarray(level))


def _mlp_kernel(head_ref, lo_ref, hi_ref, fo_lo_ref, fo_hi_ref, ho_lo_ref, ho_hi_ref, wo_ref,
                nw_ref, wg_ref, wu_ref, wd_ref, fnw_ref, out_ref, a_ref, *, first, final, g):
    if final:
        hh = pl.program_id(0) * g.nb + 1 + 2 * pl.program_id(1)
    else:
        hh = 2 * pl.program_id(0)
    h = _stream_rows(head_ref, lo_ref, hi_ref, hh, first, g)
    fo = jnp.concatenate([fo_lo_ref[0], fo_hi_ref[0]], axis=0)
    ho = jnp.concatenate([ho_lo_ref[0], ho_hi_ref[0]], axis=0)
    mixed = (jnp.dot(fo, wo_ref[0:FOX_WIDTH, :], preferred_element_type=F32)
             + jnp.dot(ho, wo_ref[FOX_WIDTH:, :], preferred_element_type=F32))
    h1 = h + mixed
    u = _rms(h1, nw_ref[...])
    dff = wg_ref.shape[1]
    for c0 in range(0, dff, FF_TILE):
        cs = slice(c0, c0 + FF_TILE)
        gate = jnp.dot(u, wg_ref[:, cs], preferred_element_type=F32)
        up = jnp.dot(u, wu_ref[:, cs], preferred_element_type=F32)
        a_ref[:, cs] = (_silu(gate) * up).astype(BF16)
    h2 = h1 + jnp.dot(a_ref[...], wd_ref[...], preferred_element_type=F32)
    if final:
        h2 = _rms(h2, fnw_ref[...])
    else:
        row = lax.broadcasted_iota(jnp.int32, (ROW_TILE, 1), 0)
        is_pad = (((row < g.pad) & (hh % g.nb == 0))
                  | ((row >= SEQ_TILE) & (row < SEQ_TILE + g.pad) & ((hh + 1) % g.nb == 0)))
        h2 = jnp.where(is_pad, 0.0, h2)
    out_ref[...] = h2


def _mlp(head, stream, first, g, layer, fo, ho, wo, nw, wg, wu, wd, fnw, final):
    d = head.shape[1]
    dff = wg.shape[-1]
    if final:
        grid = (g.batch, g.sb // 2)
        block_of = lambda b, k: b * g.nb + 1 + 2 * k
        out_rows = g.batch * g.sb * SEQ_TILE
        out_spec = pl.BlockSpec((ROW_TILE, d), lambda b, k: (b * (g.sb // 2) + k, 0))
    else:
        grid = (g.batch * g.nb // 2,)
        block_of = lambda i: 2 * i
        out_rows = g.batch * g.nb * SEQ_TILE
        out_spec = pl.BlockSpec((ROW_TILE, d), lambda i: (i, 0))
    const = lambda a: _layer_spec(a, layer, True)
    small = lambda shape: pl.BlockSpec(shape, lambda *_: (0, 0))
    blocks = lambda a: a.reshape(-1, SEQ_TILE, a.shape[-1])
    fo3, ho3 = blocks(fo), blocks(ho)
    return pl.pallas_call(
        functools.partial(_mlp_kernel, first=first, final=final, g=g),
        grid=grid,
        in_specs=[small(head.shape),
                  *_stream_specs(first, g, d, block_of),
                  *_stream_specs(False, g, FOX_WIDTH, block_of),
                  *_stream_specs(False, g, HG_WIDTH, block_of),
                  const(wo), _layer_spec(nw, layer), const(wg), const(wu), const(wd),
                  small((1, d))],
        out_specs=out_spec,
        out_shape=jax.ShapeDtypeStruct((out_rows, d), F32),
        scratch_shapes=[pltpu.VMEM((ROW_TILE, dff), BF16)],
        compiler_params=pltpu.CompilerParams(
            dimension_semantics=("arbitrary",) * len(grid), vmem_limit_bytes=VMEM_LIMIT),
        name="mlp_final" if final else "mlp",
    )(head, stream, stream, fo3, fo3, ho3, ho3, wo, nw, wg, wu, wd, fnw)


def kernel(x, meta, norm_mix_w, w_in, fox_f_bias, hgrn_lb_raw, hgrn_norm_w, w_out,
           norm_ffn_w, w_ffn_gate, w_ffn_up, w_ffn_down, norm_final_w):
    batch, seq, d = x.shape
    depth = w_in.shape[0]
    assert seq % ROW_TILE == 0 and N_META <= SEQ_TILE
    lp = seq + SEQ_TILE
    assert (batch * lp) % ROW_TILE == 0
    padf = SEQ_TILE - N_META
    g = _Geom(batch=batch, nb=lp // SEQ_TILE, sb=seq // SEQ_TILE, pad=padf)

    head = jnp.concatenate([jnp.zeros((padf, d), x.dtype), meta.astype(x.dtype)], axis=0)
    stream = x.reshape(batch * g.sb, SEQ_TILE, d)

    wfox, whg, wff = _split_w_in(w_in.astype(F32))
    wo, wg, wu, wd = (w.astype(F32) for w in (w_out, w_ffn_gate, w_ffn_up, w_ffn_down))
    fb = jnp.pad(fox_f_bias.astype(F32), ((0, 0), (0, LANES - FOX_HEADS)))[:, None]
    lb_raw = hgrn_lb_raw.astype(F32)
    nw_mix, nw_ffn, nw_hg = (a.astype(F32)[:, None]
                             for a in (norm_mix_w, norm_ffn_w, hgrn_norm_w))
    for l in range(depth):
        first, final = l == 0, l == depth - 1
        qt, k, vt, hq, hf, hi, hg, ff, nrm = _inproj(head, stream, first, g, l, nw_mix,
                                                     wfox, whg, wff)
        e, js = _gates(ff, fb, nrm, l, batch, lp, padf)
        fox_o = _fox(js, qt, k, vt, e, batch, lp)
        hgrn_o = _hgrn(hq, hf, hi, hg, lb_raw, nw_hg, l, batch, lp)
        out = _mlp(head, stream, first, g, l, fox_o, hgrn_o, wo, nw_ffn, wg, wu, wd,
                   norm_final_w[None].astype(F32), final)
        stream = out.reshape(-1, SEQ_TILE, d)
    return out.reshape(batch, seq, d)
```

```python
import functools
from typing import NamedTuple

import numpy as np
import jax
import jax.numpy as jnp
from jax import lax
from jax.experimental import pallas as pl
from jax.experimental.pallas import tpu as pltpu

F32 = jnp.float32
BF16 = jnp.bfloat16

N_META = 16
FOX_HEADS = 8
FOX_HEAD_DIM = 64
FOX_WIDTH = FOX_HEADS * FOX_HEAD_DIM
HG_HEADS = 4
HG_K = 128
HG_V = 128
HG_WIDTH = HG_HEADS * HG_K
CHUNK = 64
EPS = 1e-6
MASK_VALUE = -1e30
LOG_F_MIN = -30.0
LOG2E = 1.4426950408889634
SKIP_LOG2 = 140.0
NORM_SLACK = 1.02

LANES = 128
SEQ_TILE = 256
ROW_TILE = 512
FF_TILE = 256
VMEM_LIMIT = 56 * 1024 * 1024

N_LEVELS = 6
N_EXP_ROWS = (2 + N_LEVELS) * CHUNK


def _rms(x, w):
    ms = jnp.mean(x * x, axis=-1, keepdims=True)
    return x * lax.rsqrt(ms + EPS) * w


def _split3(x):
    hi = x.astype(BF16)
    r1 = x - hi.astype(F32)
    mid = r1.astype(BF16)
    lo = (r1 - mid.astype(F32)).astype(BF16)
    return hi, mid, lo


def _silu(x, scale=1.0):
    return (scale * x) / (1.0 + jnp.exp2(-LOG2E * x))


def _split_w_in_kernel(wt_ref, fox_ref, hg_ref, ff_ref):
    fw = 3 * FOX_WIDTH
    t = SEQ_TILE

    def block(r0, rows=t):
        return wt_ref[r0:r0 + rows, :]

    for j in range(fw // t):
        fox_ref[:, j * t:(j + 1) * t] = block(j * t).T.astype(BF16)
    for j in range(hg_ref.shape[1] // t):
        hg_ref[:, j * t:(j + 1) * t] = block(fw + FOX_HEADS + j * t).T.astype(BF16)
    ffb = block(fw, LANES)
    row = lax.broadcasted_iota(jnp.int32, ffb.shape, 0)
    ff_ref[...] = jnp.where(row < FOX_HEADS, ffb, 0.0).T.astype(BF16)


def _split_w_in(w):
    depth, kdim, n = w.shape
    fw = 3 * FOX_WIDTH
    widths = (fw, n - fw - FOX_HEADS, LANES)
    wt = jnp.transpose(w, (0, 2, 1))
    return pl.pallas_call(
        _split_w_in_kernel,
        grid=(depth,),
        in_specs=[pl.BlockSpec((None, n, kdim), lambda l: (l, 0, 0))],
        out_specs=tuple(pl.BlockSpec((None, kdim, c), lambda l: (l, 0, 0)) for c in widths),
        out_shape=tuple(jax.ShapeDtypeStruct((depth, kdim, c), BF16) for c in widths),
        compiler_params=pltpu.CompilerParams(
            dimension_semantics=("arbitrary",), vmem_limit_bytes=VMEM_LIMIT),
        name="split_w_in",
    )(wt)


def _layer_spec(a, layer, resident=False):
    tail = a.shape[1:]
    index = lambda *_: (layer,) + (0,) * len(tail)
    if resident:
        return pl.BlockSpec((None,) + tail, index, pipeline_mode=pl.Buffered(1))
    return pl.BlockSpec((None,) + tail, index)


class _Geom(NamedTuple):
    batch: int
    nb: int
    sb: int
    pad: int


def _stream_specs(first, g, d, block_of):
    def spec(half):
        def index(*idx):
            hh = block_of(*idx) + half
            if first:
                hh = (hh // g.nb) * g.sb + jnp.maximum(hh % g.nb - 1, 0)
            return (hh, 0, 0)
        return pl.BlockSpec((1, SEQ_TILE, d), index)
    return [spec(0), spec(1)]


def _stream_rows(head_ref, lo_ref, hi_ref, hh, first, g):
    lo, hi = lo_ref[0], hi_ref[0]
    if first:
        lo = jnp.where(hh % g.nb == 0, head_ref[...], lo)
        hi = jnp.where((hh + 1) % g.nb == 0, head_ref[...], hi)
    return jnp.concatenate([lo, hi], axis=0)


def _head_norm_max(xt):
    sq = xt * xt
    lane = lax.broadcasted_iota(jnp.int32, (1, LANES), 1)
    out = jnp.zeros((1, LANES), F32)
    for h in range(FOX_HEADS):
        n2 = jnp.sum(sq[h * FOX_HEAD_DIM:(h + 1) * FOX_HEAD_DIM], axis=0, keepdims=True)
        out = jnp.where(lane == h, jnp.max(n2, axis=1, keepdims=True), out)
    return out


def _inproj_kernel(head_ref, lo_ref, hi_ref, nw_ref, wfox_ref, whg_ref, wff_ref, qt_ref, k_ref,
                   vt_ref, hq_ref, hf_ref, hi_out_ref, hg_ref, ff_ref, nrm_ref, *, first, g):
    h = _stream_rows(head_ref, lo_ref, hi_ref, 2 * pl.program_id(0), first, g)
    u = _rms(h, nw_ref[...]).astype(BF16)

    def mm(c0, c1, w_ref=wfox_ref):
        return jnp.dot(u, w_ref[:, c0:c1], preferred_element_type=F32)

    w = FOX_WIDTH
    q = mm(0, w) * (FOX_HEAD_DIM ** -0.5 * LOG2E)
    kf = mm(w, 2 * w)
    v = mm(2 * w, 3 * w)
    k_ref[...] = kf.astype(BF16)
    for s in range(ROW_TILE // SEQ_TILE):
        rows = slice(s * SEQ_TILE, (s + 1) * SEQ_TILE)
        qts = q[rows].T
        qt_ref[s] = qts.astype(BF16)
        vt_ref[s] = v[rows].T.astype(BF16)
        nrm_ref[s] = jnp.concatenate(
            [_head_norm_max(qts), _head_norm_max(kf[rows].T), jnp.zeros((6, LANES), F32)], axis=0)
    hq_ref[...] = mm(0, HG_WIDTH, whg_ref).astype(BF16)
    hf_ref[...] = mm(HG_WIDTH, 2 * HG_WIDTH, whg_ref)
    hi_out_ref[...] = mm(2 * HG_WIDTH, 3 * HG_WIDTH, whg_ref).astype(BF16)
    hg_ref[...] = mm(3 * HG_WIDTH, 4 * HG_WIDTH, whg_ref).astype(BF16)
    ff_ref[...] = mm(0, LANES, wff_ref)


def _inproj(head, stream, first, g, layer, nw, wfox, whg, wff):
    d = head.shape[1]
    rows = g.batch * g.nb * SEQ_TILE
    grid = (rows // ROW_TILE,)
    row_spec = lambda c: pl.BlockSpec((ROW_TILE, c), lambda i: (i, 0))
    slabs = ROW_TILE // SEQ_TILE
    slab_shape = jax.ShapeDtypeStruct((rows // SEQ_TILE, FOX_WIDTH, SEQ_TILE), BF16)
    slab_spec = pl.BlockSpec((slabs, FOX_WIDTH, SEQ_TILE), lambda i: (i, 0, 0))
    out_shape = (
        slab_shape,
        jax.ShapeDtypeStruct((rows, FOX_WIDTH), BF16),
        slab_shape,
        jax.ShapeDtypeStruct((rows, HG_WIDTH), BF16),
        jax.ShapeDtypeStruct((rows, HG_WIDTH), F32),
        jax.ShapeDtypeStruct((rows, HG_WIDTH), BF16),
        jax.ShapeDtypeStruct((rows, HG_WIDTH), BF16),
        jax.ShapeDtypeStruct((rows, LANES), F32),
        jax.ShapeDtypeStruct((rows // SEQ_TILE, 8, LANES), F32),
    )
    norm_spec = pl.BlockSpec((slabs, 8, LANES), lambda i: (i, 0, 0))
    return pl.pallas_call(
        functools.partial(_inproj_kernel, first=first, g=g),
        grid=grid,
        in_specs=[
            pl.BlockSpec(head.shape, lambda i: (0, 0)),
            *_stream_specs(first, g, d, lambda i: 2 * i),
            _layer_spec(nw, layer),
            _layer_spec(wfox, layer, True), _layer_spec(whg, layer, True),
            _layer_spec(wff, layer, True),
        ],
        out_specs=tuple(row_spec(s.shape[1]) if len(s.shape) == 2
                        else (norm_spec if s.dtype == F32 else slab_spec) for s in out_shape),
        out_shape=out_shape,
        compiler_params=pltpu.CompilerParams(
            dimension_semantics=("arbitrary",), vmem_limit_bytes=VMEM_LIMIT),
        name="inproj",
    )(head, stream, stream, nw, wfox, whg, wff)


def _gates_kernel(ff_ref, fb_ref, nrm_ref, e_ref, js_ref, *, padf):
    lp = ff_ref.shape[0]
    t = SEQ_TILE
    nblk = lp // t
    nrow = -(-nblk // 8) * 8
    h8 = FOX_HEADS
    row = lax.broadcasted_iota(jnp.int32, (t, t), 0)
    col = lax.broadcasted_iota(jnp.int32, (t, t), 1)
    tri = (row >= col).astype(BF16)
    lane = lax.broadcasted_iota(jnp.int32, (t, LANES), 1)
    rloc = lax.broadcasted_iota(jnp.int32, (t, LANES), 0)
    brow = lax.broadcasted_iota(jnp.int32, (nrow, LANES), 0)
    blane = lax.broadcasted_iota(jnp.int32, (nrow, LANES), 1)
    carry = jnp.zeros((1, LANES), F32)
    bmax = jnp.zeros((nrow, LANES), F32)
    bmin = jnp.zeros((nrow, LANES), F32)
    qnorm = jnp.zeros((nrow, LANES), F32)
    knorm = jnp.zeros((nrow, LANES), F32)
    for blk in range(nblk):
        raw = ff_ref[blk * t:(blk + 1) * t, :]
        z = raw + fb_ref[...]
        logf = jnp.minimum(z, 0.0) - jnp.log1p(jnp.exp(-jnp.abs(z)))
        hi, mid, lo = _split3(logf)
        c = (jnp.dot(tri, hi, preferred_element_type=F32)
             + jnp.dot(tri, mid, preferred_element_type=F32)
             + jnp.dot(tri, lo, preferred_element_type=F32)) + carry
        carry = c[t - 1:t, :]
        bias = jnp.where(rloc + blk * t >= padf, -LOG2E * c, MASK_VALUE)
        bh, bm, bl = _split3(bias)
        packed = jnp.where(
            lane < h8, bh.astype(F32),
            jnp.where(lane < 2 * h8, pltpu.roll(bm.astype(F32), h8, 1),
                      jnp.where(lane < 3 * h8, pltpu.roll(bl.astype(F32), 2 * h8, 1), 0.0)))
        e_ref[blk * t:(blk + 1) * t, :] = packed.astype(BF16)

        here = brow == blk
        bmax = jnp.where(here, jnp.max(bias, axis=0, keepdims=True), bmax)
        bmin = jnp.where(here, jnp.min(bias, axis=0, keepdims=True), bmin)
        qnorm = jnp.where(here, jnp.sqrt(nrm_ref[blk, 0:1, :] * NORM_SLACK), qnorm)
        knorm = jnp.where(here, jnp.sqrt(nrm_ref[blk, 1:2, :] * NORM_SLACK), knorm)

    jlane = lax.broadcasted_iota(jnp.int32, js_ref.shape, 1)
    js = jnp.zeros(js_ref.shape, jnp.int32)
    rowf = brow[:, 0:1].astype(F32)
    for i in range(nblk):
        bound = qnorm[i:i + 1] * (knorm + knorm[i:i + 1]) + bmax - bmin[i:i + 1]
        needed = jnp.where((blane < h8) & (bound > -SKIP_LOG2), 1.0, 0.0)
        needed = jnp.max(needed, axis=1, keepdims=True)
        first = jnp.min(jnp.where((needed > 0.0) | (rowf >= i), rowf, float(nrow)),
                        axis=0, keepdims=True)
        js = jnp.where(jlane == i, first.astype(jnp.int32), js)
    js_ref[...] = js


def _gates(ff, fb, nrm, layer, batch, lp, padf):
    return pl.pallas_call(
        functools.partial(_gates_kernel, padf=padf),
        grid=(batch,),
        in_specs=[pl.BlockSpec((lp, LANES), lambda b: (b, 0)),
                  _layer_spec(fb, layer),
                  pl.BlockSpec((lp // SEQ_TILE, 8, LANES), lambda b: (b, 0, 0))],
        out_specs=(pl.BlockSpec((lp, LANES), lambda b: (b, 0)),
                   pl.BlockSpec((8, LANES), lambda b: (b, 0))),
        out_shape=(jax.ShapeDtypeStruct((batch * lp, LANES), BF16),
                   jax.ShapeDtypeStruct((batch * 8, LANES), jnp.int32)),
        compiler_params=pltpu.CompilerParams(
            dimension_semantics=("arbitrary",), vmem_limit_bytes=VMEM_LIMIT),
        name="fox_gates",
    )(ff, fb, nrm)


def _fox_kernel(js_ref, qt_ref, k_ref, vt_ref, e_ref, o_ref):
    i = pl.program_id(1)
    t = SEQ_TILE
    half = FOX_HEAD_DIM
    npair = FOX_HEADS // 2
    n_one = 16
    srow = lax.broadcasted_iota(jnp.int32, (LANES, t), 0)
    key_i = lax.broadcasted_iota(jnp.int32, (t, 2 * t), 0)
    qry_i = lax.broadcasted_iota(jnp.int32, (t, 2 * t), 1)
    causal = key_i <= jnp.where(qry_i >= t, qry_i - t, qry_i)
    lanes_of = lambda p: slice(p * LANES, (p + 1) * LANES)

    def stacked_qt(p):
        top = qt_ref[0, lanes_of(p), :]
        zero = jnp.zeros_like(top)
        sel_a = ((srow == 2 * p) | (srow == FOX_HEADS + 2 * p)
                 | (srow == 2 * FOX_HEADS + 2 * p)).astype(BF16)
        sel_b = ((srow == 2 * p + 1) | (srow == FOX_HEADS + 2 * p + 1)
                 | (srow == 2 * FOX_HEADS + 2 * p + 1)).astype(BF16)
        col_a = jnp.concatenate([jnp.where(srow < half, top, zero), sel_a], axis=0)
        col_b = jnp.concatenate([jnp.where(srow >= half, top, zero), sel_b], axis=0)
        return jnp.concatenate([col_a, col_b], axis=1)

    qqt = [stacked_qt(p) for p in range(npair)]

    def scores_of(j, nblk, p, masked):
        rows = pl.ds(pl.multiple_of(j * t, t), nblk * t)
        kaug = jnp.concatenate([k_ref[rows, lanes_of(p)], e_ref[rows, :]], axis=1)
        st = jnp.dot(kaug, qqt[p], preferred_element_type=F32)
        return jnp.where(causal, st, MASK_VALUE) if masked else st

    def softmax_of(m_old, st):
        m_new = jnp.maximum(m_old, jnp.max(st, axis=0, keepdims=True))
        return m_new, jnp.exp2(m_old - m_new), jnp.exp2(st - m_new).astype(BF16)

    def pv_of(j, nblk, p, pr):
        vt = jnp.concatenate([vt_ref[j + b, lanes_of(p), :] for b in range(nblk)], axis=1)
        vaug = jnp.concatenate([vt, jnp.ones((n_one, nblk * t), BF16)], axis=0)
        return jnp.dot(vaug, pr, preferred_element_type=F32)

    def step(j, nblk, carry, masked):
        scores = [scores_of(j, nblk, p, masked) for p in range(npair)]
        probs = [softmax_of(carry[p][0], scores[p]) for p in range(npair)]
        return tuple((probs[p][0], probs[p][1] * carry[p][1] + pv_of(j, nblk, p, probs[p][2]))
                     for p in range(npair))

    init = tuple((jnp.full((1, 2 * t), -jnp.inf, F32), jnp.zeros((LANES + n_one, 2 * t), F32))
                 for _ in range(npair))
    j0 = js_ref[pl.program_id(0) * 8, i]
    n_vis = i - j0
    carry = lax.fori_loop(0, n_vis // 2, lambda jj, c: step(j0 + 2 * jj, 2, c, False), init)
    carry = lax.fori_loop(0, n_vis % 2, lambda _, c: step(i - 1, 1, c, False), carry)
    fin = step(i, 1, carry, True)
    for p in range(npair):
        acc = fin[p][1]
        ot = acc[0:LANES] / acc[LANES:LANES + 1]
        pair_t = jnp.concatenate([ot[0:half, 0:t], ot[half:LANES, t:2 * t]], axis=0)
        o_ref[:, lanes_of(p)] = pair_t.T.astype(o_ref.dtype)


def _fox(js, qt, k, vt, e, batch, lp):
    nq = lp // SEQ_TILE
    grid_spec = pltpu.PrefetchScalarGridSpec(
        num_scalar_prefetch=1,
        grid=(batch, nq),
        in_specs=[
            pl.BlockSpec((1, FOX_WIDTH, SEQ_TILE), lambda b, i, js: (b * nq + i, 0, 0)),
            pl.BlockSpec((lp, FOX_WIDTH), lambda b, i, js: (b, 0)),
            pl.BlockSpec((nq, FOX_WIDTH, SEQ_TILE), lambda b, i, js: (b, 0, 0)),
            pl.BlockSpec((lp, LANES), lambda b, i, js: (b, 0)),
        ],
        out_specs=pl.BlockSpec((SEQ_TILE, FOX_WIDTH), lambda b, i, js: (b * nq + i, 0)),
    )
    return pl.pallas_call(
        _fox_kernel,
        grid_spec=grid_spec,
        out_shape=jax.ShapeDtypeStruct((batch * lp, FOX_WIDTH), BF16),
        compiler_params=pltpu.CompilerParams(
            dimension_semantics=("arbitrary", "arbitrary"), vmem_limit_bytes=VMEM_LIMIT),
        name="fox_attention",
    )(js, qt, k, vt, e)


def _hgrn_constants():
    c = CHUNK
    m = np.zeros((N_EXP_ROWS, c), np.float32)
    upper = np.zeros((N_LEVELS, c), np.float32)
    for t in range(c):
        m[t, :t + 1] = 1.0
        m[c + t, t + 1:] = 1.0
        for lv in range(N_LEVELS):
            w = c >> (lv + 1)
            mid = (t // (2 * w)) * 2 * w + w
            if t >= mid:
                m[(2 + lv) * c + t, mid:t + 1] = 1.0
                upper[lv, t] = 1.0
            else:
                m[(2 + lv) * c + t, t + 1:mid] = 1.0
    level = np.full((c, c), N_LEVELS + 1, np.int32)
    for t in range(c):
        level[t, t] = N_LEVELS
        for s in range(t):
            w = 1 << int(np.floor(np.log2(t ^ s)))
            level[t, s] = int(np.log2(c // (2 * w)))
    m3 = np.concatenate([m, m, m], axis=1)
    return m3, upper, level


def _hgrn_kernel(hq_ref, hf_ref, hi_ref, hg_ref, lbraw_ref, nw_ref, m3_ref, up_ref, lvl_ref,
                 o_ref, st_ref, *, layer):
    @pl.when(pl.program_id(1) == 0)
    def _():
        st_ref[...] = jnp.zeros_like(st_ref)

    c = CHUNK
    raw = lbraw_ref[...]
    ex = jnp.exp(raw - jnp.max(raw, axis=0, keepdims=True))
    s_lb = ex / jnp.sum(ex, axis=0, keepdims=True)
    lb = jnp.sum(s_lb[:layer + 1], axis=0, keepdims=True) - s_lb[0:1]
    level = jnp.concatenate([lvl_ref[...], lvl_ref[...]], axis=1)
    nt = (((1,), (1,)), ((), ()))
    tn = (((0,), (0,)), ((), ()))
    n_chunks = hq_ref.shape[0] // c
    heads = [slice(h * HG_K, (h + 1) * HG_K) for h in range(HG_HEADS)]

    one_m_lb = 1.0 - lb
    floor2 = LOG_F_MIN * LOG2E
    level_is = [level == lv for lv in range(N_LEVELS + 1)]
    gates, keys, queries = [], [], []
    for ci in range(n_chunks):
        z = hf_ref[ci * c:(ci + 1) * c, :]
        omf = one_m_lb / (1.0 + jnp.exp2(LOG2E * z))
        lg2 = jnp.log2(1.0 - omf)
        gates.append(jnp.maximum(lg2, floor2))
        keys.append(jnp.where(lg2 >= floor2, omf, 1.0 - np.exp(LOG_F_MIN)))
        queries.append(_silu(hq_ref[ci * c:(ci + 1) * c, :].astype(F32), HG_K ** -0.5))
    exps = [jnp.dot(m3_ref[...], jnp.concatenate(_split3(g), axis=0),
                    preferred_element_type=F32) for g in gates]

    def pair_diag(x, pp):
        h0, h1 = heads[2 * pp], heads[2 * pp + 1]
        zero = jnp.zeros((c, HG_K), x.dtype)
        return jnp.concatenate([jnp.concatenate([x[:, h0], zero], axis=1),
                                jnp.concatenate([zero, x[:, h1]], axis=1)], axis=0)

    q_dec, k_dec, dec_last, values, a_mats = [], [], [], [], []
    for ci in range(n_chunks):
        q, kk, e_all = queries[ci], keys[ci], exps[ci]
        values.append(hi_ref[ci * c:(ci + 1) * c, :])
        b = e_all[0:c]
        q_dec.append((q * jnp.exp2(b)).astype(BF16))
        k_dec.append((kk * jnp.exp2(e_all[c:2 * c])).astype(BF16))
        dec_last.append(jnp.exp2(b[c - 1:c, :]))
        rl = [(jnp.where(up_ref[lv] > 0.5, q, kk)
               * jnp.exp2(e_all[(2 + lv) * c:(3 + lv) * c])).astype(BF16)
              for lv in range(N_LEVELS)]
        qb, kb = q.astype(BF16), kk.astype(BF16)
        per_pair = []
        for pp in range(HG_HEADS // 2):
            both = slice(2 * pp * HG_K, (2 * pp + 2) * HG_K)
            a = jnp.where(level_is[N_LEVELS],
                          lax.dot_general(qb[:, both], pair_diag(kb, pp), nt,
                                          preferred_element_type=F32), 0.0)
            for lv in range(N_LEVELS):
                a = jnp.where(level_is[lv],
                              lax.dot_general(rl[lv][:, both], pair_diag(rl[lv], pp), nt,
                                              preferred_element_type=F32), a)
            per_pair.append(a.astype(BF16))
        a_mats.append(per_pair)
    o_intra = [jnp.concatenate(
        [jnp.dot(a_mats[ci][pp], pair_diag(values[ci], pp), preferred_element_type=F32)
         for pp in range(HG_HEADS // 2)], axis=1) for ci in range(n_chunks)]
    delta = [[lax.dot_general(values[ci][:, hs], k_dec[ci][:, hs], tn,
                              preferred_element_type=F32) for hs in heads]
             for ci in range(n_chunks)]

    states = [st_ref[h] for h in range(HG_HEADS)]
    for ci in range(n_chunks):
        rows = slice(ci * c, (ci + 1) * c)
        gate = _silu(hg_ref[rows, :].astype(F32))
        for h, hs in enumerate(heads):
            st = states[h]
            o = o_intra[ci][:, hs] + lax.dot_general(q_dec[ci][:, hs], st.astype(BF16), nt,
                                                     preferred_element_type=F32)
            states[h] = st * dec_last[ci][:, hs] + delta[ci][h]
            o_ref[rows, hs] = (_rms(o, nw_ref[...]) * gate[:, hs]).astype(o_ref.dtype)
    for h in range(HG_HEADS):
        st_ref[h] = states[h]


def _hgrn(hq, hf, hi, hg, lb_raw, nw, layer, batch, lp):
    nb = lp // SEQ_TILE
    m3, upper, level = _hgrn_constants()
    row_spec = pl.BlockSpec((SEQ_TILE, HG_WIDTH), lambda b, i: (b * nb + i, 0))
    full = lambda shape: pl.BlockSpec(shape, lambda b, i: (0,) * len(shape))
    depth = lb_raw.shape[0]
    return pl.pallas_call(
        functools.partial(_hgrn_kernel, layer=layer),
        grid=(batch, nb),
        in_specs=[row_spec, row_spec, row_spec, row_spec,
                  full((depth, HG_WIDTH)), _layer_spec(nw, layer),
                  full(m3.shape), full((N_LEVELS, CHUNK, 1)), full(level.shape)],
        out_specs=row_spec,
        out_shape=jax.ShapeDtypeStruct((batch * lp, HG_WIDTH), BF16),
        scratch_shapes=[pltpu.VMEM((HG_HEADS, HG_V, HG_K), F32)],
        compiler_params=pltpu.CompilerParams(
            dimension_semantics=("arbitrary", "arbitrary"), vmem_limit_bytes=VMEM_LIMIT),
        name="hgrn2",
    )(hq, hf, hi, hg, lb_raw, nw, jnp.asarray(m3, BF16),
      jnp.asarray(upper[:, :, None], F32), jnp.asarray(level))


def _mlp_kernel(head_ref, lo_ref, hi_ref, fo_lo_ref, fo_hi_ref, ho_lo_ref, ho_hi_ref, wo_ref,
                nw_ref, wg_ref, wu_ref, wd_ref, fnw_ref, out_ref, a_ref, *, first, final, g):
    if final:
        hh = pl.program_id(0) * g.nb + 1 + 2 * pl.program_id(1)
    else:
        hh = 2 * pl.program_id(0)
    h = _stream_rows(head_ref, lo_ref, hi_ref, hh, first, g)
    fo = jnp.concatenate([fo_lo_ref[0], fo_hi_ref[0]], axis=0)
    ho = jnp.concatenate([ho_lo_ref[0], ho_hi_ref[0]], axis=0)
    mixed = (jnp.dot(fo, wo_ref[0:FOX_WIDTH, :], preferred_element_type=F32)
             + jnp.dot(ho, wo_ref[FOX_WIDTH:, :], preferred_element_type=F32))
    h1 = h + mixed
    u = _rms(h1, nw_ref[...])
    dff = wg_ref.shape[1]
    for c0 in range(0, dff, FF_TILE):
        cs = slice(c0, c0 + FF_TILE)
        gate = jnp.dot(u, wg_ref[:, cs], preferred_element_type=F32)
        up = jnp.dot(u, wu_ref[:, cs], preferred_element_type=F32)
        a_ref[:, cs] = (_silu(gate) * up).astype(BF16)
    h2 = h1 + jnp.dot(a_ref[...], wd_ref[...], preferred_element_type=F32)
    if final:
        h2 = _rms(h2, fnw_ref[...])
    else:
        row = lax.broadcasted_iota(jnp.int32, (ROW_TILE, 1), 0)
        is_pad = (((row < g.pad) & (hh % g.nb == 0))
                  | ((row >= SEQ_TILE) & (row < SEQ_TILE + g.pad) & ((hh + 1) % g.nb == 0)))
        h2 = jnp.where(is_pad, 0.0, h2)
    out_ref[...] = h2


def _mlp(head, stream, first, g, layer, fo, ho, wo, nw, wg, wu, wd, fnw, final):
    d = head.shape[1]
    dff = wg.shape[-1]
    if final:
        grid = (g.batch, g.sb // 2)
        block_of = lambda b, k: b * g.nb + 1 + 2 * k
        out_rows = g.batch * g.sb * SEQ_TILE
        out_spec = pl.BlockSpec((ROW_TILE, d), lambda b, k: (b * (g.sb // 2) + k, 0))
    else:
        grid = (g.batch * g.nb // 2,)
        block_of = lambda i: 2 * i
        out_rows = g.batch * g.nb * SEQ_TILE
        out_spec = pl.BlockSpec((ROW_TILE, d), lambda i: (i, 0))
    const = lambda a: _layer_spec(a, layer, True)
    small = lambda shape: pl.BlockSpec(shape, lambda *_: (0, 0))
    blocks = lambda a: a.reshape(-1, SEQ_TILE, a.shape[-1])
    fo3, ho3 = blocks(fo), blocks(ho)
    return pl.pallas_call(
        functools.partial(_mlp_kernel, first=first, final=final, g=g),
        grid=grid,
        in_specs=[small(head.shape),
                  *_stream_specs(first, g, d, block_of),
                  *_stream_specs(False, g, FOX_WIDTH, block_of),
                  *_stream_specs(False, g, HG_WIDTH, block_of),
                  const(wo), _layer_spec(nw, layer), const(wg), const(wu), const(wd),
                  small((1, d))],
        out_specs=out_spec,
        out_shape=jax.ShapeDtypeStruct((out_rows, d), F32),
        scratch_shapes=[pltpu.VMEM((ROW_TILE, dff), BF16)],
        compiler_params=pltpu.CompilerParams(
            dimension_semantics=("arbitrary",) * len(grid), vmem_limit_bytes=VMEM_LIMIT),
        name="mlp_final" if final else "mlp",
    )(head, stream, stream, fo3, fo3, ho3, ho3, wo, nw, wg, wu, wd, fnw)


def kernel(x, meta, norm_mix_w, w_in, fox_f_bias, hgrn_lb_raw, hgrn_norm_w, w_out,
           norm_ffn_w, w_ffn_gate, w_ffn_up, w_ffn_down, norm_final_w):
    batch, seq, d = x.shape
    depth = w_in.shape[0]
    assert seq % ROW_TILE == 0 and N_META <= SEQ_TILE
    lp = seq + SEQ_TILE
    assert (batch * lp) % ROW_TILE == 0
    padf = SEQ_TILE - N_META
    g = _Geom(batch=batch, nb=lp // SEQ_TILE, sb=seq // SEQ_TILE, pad=padf)

    head = jnp.concatenate([jnp.zeros((padf, d), x.dtype), meta.astype(x.dtype)], axis=0)
    stream = x.reshape(batch * g.sb, SEQ_TILE, d)

    wfox, whg, wff = _split_w_in(w_in.astype(F32))
    wo, wg, wu, wd = (w.astype(F32) for w in (w_out, w_ffn_gate, w_ffn_up, w_ffn_down))
    fb = jnp.pad(fox_f_bias.astype(F32), ((0, 0), (0, LANES - FOX_HEADS)))[:, None]
    lb_raw = hgrn_lb_raw.astype(F32)
    nw_mix, nw_ffn, nw_hg = (a.astype(F32)[:, None]
                             for a in (norm_mix_w, norm_ffn_w, hgrn_norm_w))
    for l in range(depth):
        first, final = l == 0, l == depth - 1
        qt, k, vt, hq, hf, hi, hg, ff, nrm = _inproj(head, stream, first, g, l, nw_mix,
                                                     wfox, whg, wff)
        e, js = _gates(ff, fb, nrm, l, batch, lp, padf)
        fox_o = _fox(js, qt, k, vt, e, batch, lp)
        hgrn_o = _hgrn(hq, hf, hi, hg, lb_raw, nw_hg, l, batch, lp)
        out = _mlp(head, stream, first, g, l, fox_o, hgrn_o, wo, nw_ffn, wg, wu, wd,
                   norm_final_w[None].astype(F32), final)
        stream = out.reshape(-1, SEQ_TILE, d)
    return out.reshape(batch, seq, d)
```

```python
import functools
from typing import NamedTuple

import numpy as np
import jax
import jax.numpy as jnp
from jax import lax
from jax.experimental import pallas as pl
from jax.experimental.pallas import tpu as pltpu

F32 = jnp.float32
BF16 = jnp.bfloat16

N_META = 16
FOX_HEADS = 8
FOX_HEAD_DIM = 64
FOX_WIDTH = FOX_HEADS * FOX_HEAD_DIM
HG_HEADS = 4
HG_K = 128
HG_V = 128
HG_WIDTH = HG_HEADS * HG_K
CHUNK = 64
EPS = 1e-6
MASK_VALUE = -1e30
LOG_F_MIN = -30.0
LOG2E = 1.4426950408889634
SKIP_LOG2 = 140.0
NORM_SLACK = 1.02

LANES = 128
SEQ_TILE = 256
ROW_TILE = 512
FF_TILE = 256
VMEM_LIMIT = 56 * 1024 * 1024

N_LEVELS = 6
N_EXP_ROWS = (2 + N_LEVELS) * CHUNK


def _rms(x, w):
    ms = jnp.mean(x * x, axis=-1, keepdims=True)
    return x * lax.rsqrt(ms + EPS) * w


def _split3(x):
    hi = x.astype(BF16)
    r1 = x - hi.astype(F32)
    mid = r1.astype(BF16)
    lo = (r1 - mid.astype(F32)).astype(BF16)
    return hi, mid, lo


def _silu(x, scale=1.0):
    return (scale * x) / (1.0 + jnp.exp2(-LOG2E * x))


def _split_w_in_kernel(wt_ref, fox_ref, hg_ref, ff_ref):
    fw = 3 * FOX_WIDTH
    t = SEQ_TILE

    def block(r0, rows=t):
        return wt_ref[r0:r0 + rows, :]

    for j in range(fw // t):
        fox_ref[:, j * t:(j + 1) * t] = block(j * t).T.astype(BF16)
    for j in range(hg_ref.shape[1] // t):
        hg_ref[:, j * t:(j + 1) * t] = block(fw + FOX_HEADS + j * t).T.astype(BF16)
    ffb = block(fw, LANES)
    row = lax.broadcasted_iota(jnp.int32, ffb.shape, 0)
    ff_ref[...] = jnp.where(row < FOX_HEADS, ffb, 0.0).T.astype(BF16)


def _split_w_in(w):
    depth, kdim, n = w.shape
    fw = 3 * FOX_WIDTH
    widths = (fw, n - fw - FOX_HEADS, LANES)
    wt = jnp.transpose(w, (0, 2, 1))
    return pl.pallas_call(
        _split_w_in_kernel,
        grid=(depth,),
        in_specs=[pl.BlockSpec((None, n, kdim), lambda l: (l, 0, 0))],
        out_specs=tuple(pl.BlockSpec((None, kdim, c), lambda l: (l, 0, 0)) for c in widths),
        out_shape=tuple(jax.ShapeDtypeStruct((depth, kdim, c), BF16) for c in widths),
        compiler_params=pltpu.CompilerParams(
            dimension_semantics=("arbitrary",), vmem_limit_bytes=VMEM_LIMIT),
        name="split_w_in",
    )(wt)


def _layer_spec(a, layer, resident=False):
    tail = a.shape[1:]
    index = lambda *_: (layer,) + (0,) * len(tail)
    if resident:
        return pl.BlockSpec((None,) + tail, index, pipeline_mode=pl.Buffered(1))
    return pl.BlockSpec((None,) + tail, index)


class _Geom(NamedTuple):
    batch: int
    nb: int
    sb: int
    pad: int


def _stream_specs(first, g, d, block_of):
    def spec(half):
        def index(*idx):
            hh = block_of(*idx) + half
            if first:
                hh = (hh // g.nb) * g.sb + jnp.maximum(hh % g.nb - 1, 0)
            return (hh, 0, 0)
        return pl.BlockSpec((1, SEQ_TILE, d), index)
    return [spec(0), spec(1)]


def _stream_rows(head_ref, lo_ref, hi_ref, hh, first, g):
    lo, hi = lo_ref[0], hi_ref[0]
    if first:
        lo = jnp.where(hh % g.nb == 0, head_ref[...], lo)
        hi = jnp.where((hh + 1) % g.nb == 0, head_ref[...], hi)
    return jnp.concatenate([lo, hi], axis=0)


def _head_norm_max(xt):
    sq = xt * xt
    lane = lax.broadcasted_iota(jnp.int32, (1, LANES), 1)
    out = jnp.zeros((1, LANES), F32)
    for h in range(FOX_HEADS):
        n2 = jnp.sum(sq[h * FOX_HEAD_DIM:(h + 1) * FOX_HEAD_DIM], axis=0, keepdims=True)
        out = jnp.where(lane == h, jnp.max(n2, axis=1, keepdims=True), out)
    return out


def _inproj_kernel(head_ref, lo_ref, hi_ref, nw_ref, wfox_ref, whg_ref, wff_ref, qt_ref, k_ref,
                   vt_ref, hq_ref, hf_ref, hi_out_ref, hg_ref, ff_ref, nrm_ref, *, first, g):
    h = _stream_rows(head_ref, lo_ref, hi_ref, 2 * pl.program_id(0), first, g)
    u = _rms(h, nw_ref[...]).astype(BF16)

    def mm(c0, c1, w_ref=wfox_ref):
        return jnp.dot(u, w_ref[:, c0:c1], preferred_element_type=F32)

    w = FOX_WIDTH
    q = mm(0, w) * (FOX_HEAD_DIM ** -0.5 * LOG2E)
    kf = mm(w, 2 * w)
    v = mm(2 * w, 3 * w)
    k_ref[...] = kf.astype(BF16)
    for s in range(ROW_TILE // SEQ_TILE):
        rows = slice(s * SEQ_TILE, (s + 1) * SEQ_TILE)
        qts = q[rows].T
        qt_ref[s] = qts.astype(BF16)
        vt_ref[s] = v[rows].T.astype(BF16)
        nrm_ref[s] = jnp.concatenate(
            [_head_norm_max(qts), _head_norm_max(kf[rows].T), jnp.zeros((6, LANES), F32)], axis=0)
    hq_ref[...] = mm(0, HG_WIDTH, whg_ref).astype(BF16)
    hf_ref[...] = mm(HG_WIDTH, 2 * HG_WIDTH, whg_ref)
    hi_out_ref[...] = mm(2 * HG_WIDTH, 3 * HG_WIDTH, whg_ref).astype(BF16)
    hg_ref[...] = mm(3 * HG_WIDTH, 4 * HG_WIDTH, whg_ref).astype(BF16)
    ff_ref[...] = mm(0, LANES, wff_ref)


def _inproj(head, stream, first, g, layer, nw, wfox, whg, wff):
    d = head.shape[1]
    rows = g.batch * g.nb * SEQ_TILE
    grid = (rows // ROW_TILE,)
    row_spec = lambda c: pl.BlockSpec((ROW_TILE, c), lambda i: (i, 0))
    slabs = ROW_TILE // SEQ_TILE
    slab_shape = jax.ShapeDtypeStruct((rows // SEQ_TILE, FOX_WIDTH, SEQ_TILE), BF16)
    slab_spec = pl.BlockSpec((slabs, FOX_WIDTH, SEQ_TILE), lambda i: (i, 0, 0))
    out_shape = (
        slab_shape,
        jax.ShapeDtypeStruct((rows, FOX_WIDTH), BF16),
        slab_shape,
        jax.ShapeDtypeStruct((rows, HG_WIDTH), BF16),
        jax.ShapeDtypeStruct((rows, HG_WIDTH), F32),
        jax.ShapeDtypeStruct((rows, HG_WIDTH), BF16),
        jax.ShapeDtypeStruct((rows, HG_WIDTH), BF16),
        jax.ShapeDtypeStruct((rows, LANES), F32),
        jax.ShapeDtypeStruct((rows // SEQ_TILE, 8, LANES), F32),
    )
    norm_spec = pl.BlockSpec((slabs, 8, LANES), lambda i: (i, 0, 0))
    return pl.pallas_call(
        functools.partial(_inproj_kernel, first=first, g=g),
        grid=grid,
        in_specs=[
            pl.BlockSpec(head.shape, lambda i: (0, 0)),
            *_stream_specs(first, g, d, lambda i: 2 * i),
            _layer_spec(nw, layer),
            _layer_spec(wfox, layer, True), _layer_spec(whg, layer, True),
            _layer_spec(wff, layer, True),
        ],
        out_specs=tuple(row_spec(s.shape[1]) if len(s.shape) == 2
                        else (norm_spec if s.dtype == F32 else slab_spec) for s in out_shape),
        out_shape=out_shape,
        compiler_params=pltpu.CompilerParams(
            dimension_semantics=("arbitrary",), vmem_limit_bytes=VMEM_LIMIT),
        name="inproj",
    )(head, stream, stream, nw, wfox, whg, wff)


def _gates_kernel(ff_ref, fb_ref, nrm_ref, e_ref, js_ref, *, padf):
    lp = ff_ref.shape[0]
    t = SEQ_TILE
    nblk = lp // t
    nrow = -(-nblk // 8) * 8
    h8 = FOX_HEADS
    row = lax.broadcasted_iota(jnp.int32, (t, t), 0)
    col = lax.broadcasted_iota(jnp.int32, (t, t), 1)
    tri = (row >= col).astype(BF16)
    lane = lax.broadcasted_iota(jnp.int32, (t, LANES), 1)
    rloc = lax.broadcasted_iota(jnp.int32, (t, LANES), 0)
    brow = lax.broadcasted_iota(jnp.int32, (nrow, LANES), 0)
    blane = lax.broadcasted_iota(jnp.int32, (nrow, LANES), 1)
    carry = jnp.zeros((1, LANES), F32)
    bmax = jnp.zeros((nrow, LANES), F32)
    bmin = jnp.zeros((nrow, LANES), F32)
    qnorm = jnp.zeros((nrow, LANES), F32)
    knorm = jnp.zeros((nrow, LANES), F32)
    for blk in range(nblk):
        raw = ff_ref[blk * t:(blk + 1) * t, :]
        z = raw + fb_ref[...]
        logf = jnp.minimum(z, 0.0) - jnp.log1p(jnp.exp(-jnp.abs(z)))
        hi, mid, lo = _split3(logf)
        c = (jnp.dot(tri, hi, preferred_element_type=F32)
             + jnp.dot(tri, mid, preferred_element_type=F32)
             + jnp.dot(tri, lo, preferred_element_type=F32)) + carry
        carry = c[t - 1:t, :]
        bias = jnp.where(rloc + blk * t >= padf, -LOG2E * c, MASK_VALUE)
        bh, bm, bl = _split3(bias)
        packed = jnp.where(
            lane < h8, bh.astype(F32),
            jnp.where(lane < 2 * h8, pltpu.roll(bm.astype(F32), h8, 1),
                      jnp.where(lane < 3 * h8, pltpu.roll(bl.astype(F32), 2 * h8, 1), 0.0)))
        e_ref[blk * t:(blk + 1) * t, :] = packed.astype(BF16)

        here = brow == blk
        bmax = jnp.where(here, jnp.max(bias, axis=0, keepdims=True), bmax)
        bmin = jnp.where(here, jnp.min(bias, axis=0, keepdims=True), bmin)
        qnorm = jnp.where(here, jnp.sqrt(nrm_ref[blk, 0:1, :] * NORM_SLACK), qnorm)
        knorm = jnp.where(here, jnp.sqrt(nrm_ref[blk, 1:2, :] * NORM_SLACK), knorm)

    jlane = lax.broadcasted_iota(jnp.int32, js_ref.shape, 1)
    js = jnp.zeros(js_ref.shape, jnp.int32)
    rowf = brow[:, 0:1].astype(F32)
    for i in range(nblk):
        bound = qnorm[i:i + 1] * (knorm + knorm[i:i + 1]) + bmax - bmin[i:i + 1]
        needed = jnp.where((blane < h8) & (bound > -SKIP_LOG2), 1.0, 0.0)
        needed = jnp.max(needed, axis=1, keepdims=True)
        first = jnp.min(jnp.where((needed > 0.0) | (rowf >= i), rowf, float(nrow)),
                        axis=0, keepdims=True)
        js = jnp.where(jlane == i, first.astype(jnp.int32), js)
    js_ref[...] = js


def _gates(ff, fb, nrm, layer, batch, lp, padf):
    return pl.pallas_call(
        functools.partial(_gates_kernel, padf=padf),
        grid=(batch,),
        in_specs=[pl.BlockSpec((lp, LANES), lambda b: (b, 0)),
                  _layer_spec(fb, layer),
                  pl.BlockSpec((lp // SEQ_TILE, 8, LANES), lambda b: (b, 0, 0))],
        out_specs=(pl.BlockSpec((lp, LANES), lambda b: (b, 0)),
                   pl.BlockSpec((8, LANES), lambda b: (b, 0))),
        out_shape=(jax.ShapeDtypeStruct((batch * lp, LANES), BF16),
                   jax.ShapeDtypeStruct((batch * 8, LANES), jnp.int32)),
        compiler_params=pltpu.CompilerParams(
            dimension_semantics=("arbitrary",), vmem_limit_bytes=VMEM_LIMIT),
        name="fox_gates",
    )(ff, fb, nrm)


def _fox_kernel(js_ref, qt_ref, k_ref, vt_ref, e_ref, o_ref):
    i = pl.program_id(1)
    t = SEQ_TILE
    half = FOX_HEAD_DIM
    npair = FOX_HEADS // 2
    n_one = 16
    srow = lax.broadcasted_iota(jnp.int32, (LANES, t), 0)
    key_i = lax.broadcasted_iota(jnp.int32, (t, 2 * t), 0)
    qry_i = lax.broadcasted_iota(jnp.int32, (t, 2 * t), 1)
    causal = key_i <= jnp.where(qry_i >= t, qry_i - t, qry_i)
    lanes_of = lambda p: slice(p * LANES, (p + 1) * LANES)

    def stacked_qt(p):
        top = qt_ref[0, lanes_of(p), :]
        zero = jnp.zeros_like(top)
        sel_a = ((srow == 2 * p) | (srow == FOX_HEADS + 2 * p)
                 | (srow == 2 * FOX_HEADS + 2 * p)).astype(BF16)
        sel_b = ((srow == 2 * p + 1) | (srow == FOX_HEADS + 2 * p + 1)
                 | (srow == 2 * FOX_HEADS + 2 * p + 1)).astype(BF16)
        col_a = jnp.concatenate([jnp.where(srow < half, top, zero), sel_a], axis=0)
        col_b = jnp.concatenate([jnp.where(srow >= half, top, zero), sel_b], axis=0)
        return jnp.concatenate([col_a, col_b], axis=1)

    qqt = [stacked_qt(p) for p in range(npair)]

    def scores_of(j, nblk, p, masked):
        rows = pl.ds(pl.multiple_of(j * t, t), nblk * t)
        kaug = jnp.concatenate([k_ref[rows, lanes_of(p)], e_ref[rows, :]], axis=1)
        st = jnp.dot(kaug, qqt[p], preferred_element_type=F32)
        return jnp.where(causal, st, MASK_VALUE) if masked else st

    def softmax_of(m_old, st):
        m_new = jnp.maximum(m_old, jnp.max(st, axis=0, keepdims=True))
        return m_new, jnp.exp2(m_old - m_new), jnp.exp2(st - m_new).astype(BF16)

    def pv_of(j, nblk, p, pr):
        vt = jnp.concatenate([vt_ref[j + b, lanes_of(p), :] for b in range(nblk)], axis=1)
        vaug = jnp.concatenate([vt, jnp.ones((n_one, nblk * t), BF16)], axis=0)
        return jnp.dot(vaug, pr, preferred_element_type=F32)

    def step(j, nblk, carry, masked):
        scores = [scores_of(j, nblk, p, masked) for p in range(npair)]
        probs = [softmax_of(carry[p][0], scores[p]) for p in range(npair)]
        return tuple((probs[p][0], probs[p][1] * carry[p][1] + pv_of(j, nblk, p, probs[p][2]))
                     for p in range(npair))

    init = tuple((jnp.full((1, 2 * t), -jnp.inf, F32), jnp.zeros((LANES + n_one, 2 * t), F32))
                 for _ in range(npair))
    j0 = js_ref[pl.program_id(0) * 8, i]
    n_vis = i - j0
    carry = lax.fori_loop(0, n_vis // 2, lambda jj, c: step(j0 + 2 * jj, 2, c, False), init)
    carry = lax.fori_loop(0, n_vis % 2, lambda _, c: step(i - 1, 1, c, False), carry)
    fin = step(i, 1, carry, True)
    for p in range(npair):
        acc = fin[p][1]
        ot = acc[0:LANES] / acc[LANES:LANES + 1]
        pair_t = jnp.concatenate([ot[0:half, 0:t], ot[half:LANES, t:2 * t]], axis=0)
        o_ref[:, lanes_of(p)] = pair_t.T.astype(o_ref.dtype)


def _fox(js, qt, k, vt, e, batch, lp):
    nq = lp // SEQ_TILE
    grid_spec = pltpu.PrefetchScalarGridSpec(
        num_scalar_prefetch=1,
        grid=(batch, nq),
        in_specs=[
            pl.BlockSpec((1, FOX_WIDTH, SEQ_TILE), lambda b, i, js: (b * nq + i, 0, 0)),
            pl.BlockSpec((lp, FOX_WIDTH), lambda b, i, js: (b, 0)),
            pl.BlockSpec((nq, FOX_WIDTH, SEQ_TILE), lambda b, i, js: (b, 0, 0)),
            pl.BlockSpec((lp, LANES), lambda b, i, js: (b, 0)),
        ],
        out_specs=pl.BlockSpec((SEQ_TILE, FOX_WIDTH), lambda b, i, js: (b * nq + i, 0)),
    )
    return pl.pallas_call(
        _fox_kernel,
        grid_spec=grid_spec,
        out_shape=jax.ShapeDtypeStruct((batch * lp, FOX_WIDTH), BF16),
        compiler_params=pltpu.CompilerParams(
            dimension_semantics=("arbitrary", "arbitrary"), vmem_limit_bytes=VMEM_LIMIT),
        name="fox_attention",
    )(js, qt, k, vt, e)


def _hgrn_constants():
    c = CHUNK
    m = np.zeros((N_EXP_ROWS, c), np.float32)
    upper = np.zeros((N_LEVELS, c), np.float32)
    for t in range(c):
        m[t, :t + 1] = 1.0
        m[c + t, t + 1:] = 1.0
        for lv in range(N_LEVELS):
            w = c >> (lv + 1)
            mid = (t // (2 * w)) * 2 * w + w
            if t >= mid:
                m[(2 + lv) * c + t, mid:t + 1] = 1.0
                upper[lv, t] = 1.0
            else:
                m[(2 + lv) * c + t, t + 1:mid] = 1.0
    level = np.full((c, c), N_LEVELS + 1, np.int32)
    for t in range(c):
        level[t, t] = N_LEVELS
        for s in range(t):
            w = 1 << int(np.floor(np.log2(t ^ s)))
            level[t, s] = int(np.log2(c // (2 * w)))
    m3 = np.concatenate([m, m, m], axis=1)
    return m3, upper, level


def _hgrn_kernel(hq_ref, hf_ref, hi_ref, hg_ref, lbraw_ref, nw_ref, m3_ref, up_ref, lvl_ref,
                 o_ref, st_ref, *, layer):
    @pl.when(pl.program_id(1) == 0)
    def _():
        st_ref[...] = jnp.zeros_like(st_ref)

    c = CHUNK
    raw = lbraw_ref[...]
    ex = jnp.exp(raw - jnp.max(raw, axis=0, keepdims=True))
    s_lb = ex / jnp.sum(ex, axis=0, keepdims=True)
    lb = jnp.sum(s_lb[:layer + 1], axis=0, keepdims=True) - s_lb[0:1]
    level = lvl_ref[...]
    nt = (((1,), (1,)), ((), ()))
    tn = (((0,), (0,)), ((), ()))
    n_chunks = hq_ref.shape[0] // c
    heads = [slice(h * HG_K, (h + 1) * HG_K) for h in range(HG_HEADS)]

    one_m_lb = 1.0 - lb
    floor2 = LOG_F_MIN * LOG2E
    level_is = [level == lv for lv in range(N_LEVELS + 1)]
    gates, keys, queries = [], [], []
    for ci in range(n_chunks):
        z = hf_ref[ci * c:(ci + 1) * c, :]
        omf = one_m_lb / (1.0 + jnp.exp2(LOG2E * z))
        lg2 = jnp.log2(1.0 - omf)
        gates.append(jnp.maximum(lg2, floor2))
        keys.append(jnp.where(lg2 >= floor2, omf, 1.0 - np.exp(LOG_F_MIN)))
        queries.append(_silu(hq_ref[ci * c:(ci + 1) * c, :].astype(F32), HG_K ** -0.5))
    exps = [jnp.dot(m3_ref[...], jnp.concatenate(_split3(g), axis=0),
                    preferred_element_type=F32) for g in gates]

    o_intra, q_dec, k_dec, dec_last = [], [], [], []
    for ci in range(n_chunks):
        q, kk, e_all = queries[ci], keys[ci], exps[ci]
        v = hi_ref[ci * c:(ci + 1) * c, :]
        b = e_all[0:c]
        q_dec.append((q * jnp.exp2(b)).astype(BF16))
        k_dec.append((kk * jnp.exp2(e_all[c:2 * c])).astype(BF16))
        dec_last.append(jnp.exp2(b[c - 1:c, :]))
        rl = [(jnp.where(up_ref[lv] > 0.5, q, kk)
               * jnp.exp2(e_all[(2 + lv) * c:(3 + lv) * c])).astype(BF16)
              for lv in range(N_LEVELS)]
        qb, kb = q.astype(BF16), kk.astype(BF16)
        per_head = []
        for hs in heads:
            a = jnp.where(level_is[N_LEVELS],
                          lax.dot_general(qb[:, hs], kb[:, hs], nt, preferred_element_type=F32),
                          0.0)
            for lv in range(N_LEVELS):
                r = rl[lv][:, hs]
                a = jnp.where(level_is[lv],
                              lax.dot_general(r, r, nt, preferred_element_type=F32), a)
            per_head.append(jnp.dot(a.astype(BF16), v[:, hs], preferred_element_type=F32))
        o_intra.append(per_head)

    states = [st_ref[h] for h in range(HG_HEADS)]
    for ci in range(n_chunks):
        rows = slice(ci * c, (ci + 1) * c)
        v = hi_ref[rows, :]
        gate = _silu(hg_ref[rows, :].astype(F32))
        for h, hs in enumerate(heads):
            st = states[h]
            o = o_intra[ci][h] + lax.dot_general(q_dec[ci][:, hs], st.astype(BF16), nt,
                                                 preferred_element_type=F32)
            states[h] = st * dec_last[ci][:, hs] + lax.dot_general(
                v[:, hs], k_dec[ci][:, hs], tn, preferred_element_type=F32)
            o_ref[rows, hs] = (_rms(o, nw_ref[...]) * gate[:, hs]).astype(o_ref.dtype)
    for h in range(HG_HEADS):
        st_ref[h] = states[h]


def _hgrn(hq, hf, hi, hg, lb_raw, nw, layer, batch, lp):
    nb = lp // SEQ_TILE
    m3, upper, level = _hgrn_constants()
    row_spec = pl.BlockSpec((SEQ_TILE, HG_WIDTH), lambda b, i: (b * nb + i, 0))
    full = lambda shape: pl.BlockSpec(shape, lambda b, i: (0,) * len(shape))
    depth = lb_raw.shape[0]
    return pl.pallas_call(
        functools.partial(_hgrn_kernel, layer=layer),
        grid=(batch, nb),
        in_specs=[row_spec, row_spec, row_spec, row_spec,
                  full((depth, HG_WIDTH)), _layer_spec(nw, layer),
                  full(m3.shape), full((N_LEVELS, CHUNK, 1)), full(level.shape)],
        out_specs=row_spec,
        out_shape=jax.ShapeDtypeStruct((batch * lp, HG_WIDTH), BF16),
        scratch_shapes=[pltpu.VMEM((HG_HEADS, HG_V, HG_K), F32)],
        compiler_params=pltpu.CompilerParams(
            dimension_semantics=("arbitrary", "arbitrary"), vmem_limit_bytes=VMEM_LIMIT),
        name="hgrn2",
    )(hq, hf, hi, hg, lb_raw, nw, jnp.asarray(m3, BF16),
      jnp.asarray(upper[:, :, None], F32), jnp.asarray(level))


def _mlp_kernel(head_ref, lo_ref, hi_ref, fo_lo_ref, fo_hi_ref, ho_lo_ref, ho_hi_ref, wo_ref,
                nw_ref, wg_ref, wu_ref, wd_ref, fnw_ref, out_ref, a_ref, *, first, final, g):
    if final:
        hh = pl.program_id(0) * g.nb + 1 + 2 * pl.program_id(1)
    else:
        hh = 2 * pl.program_id(0)
    h = _stream_rows(head_ref, lo_ref, hi_ref, hh, first, g)
    fo = jnp.concatenate([fo_lo_ref[0], fo_hi_ref[0]], axis=0)
    ho = jnp.concatenate([ho_lo_ref[0], ho_hi_ref[0]], axis=0)
    mixed = (jnp.dot(fo, wo_ref[0:FOX_WIDTH, :], preferred_element_type=F32)
             + jnp.dot(ho, wo_ref[FOX_WIDTH:, :], preferred_element_type=F32))
    h1 = h + mixed
    u = _rms(h1, nw_ref[...])
    dff = wg_ref.shape[1]
    for c0 in range(0, dff, FF_TILE):
        cs = slice(c0, c0 + FF_TILE)
        gate = jnp.dot(u, wg_ref[:, cs], preferred_element_type=F32)
        up = jnp.dot(u, wu_ref[:, cs], preferred_element_type=F32)
        a_ref[:, cs] = (_silu(gate) * up).astype(BF16)
    h2 = h1 + jnp.dot(a_ref[...], wd_ref[...], preferred_element_type=F32)
    if final:
        h2 = _rms(h2, fnw_ref[...])
    else:
        row = lax.broadcasted_iota(jnp.int32, (ROW_TILE, 1), 0)
        is_pad = (((row < g.pad) & (hh % g.nb == 0))
                  | ((row >= SEQ_TILE) & (row < SEQ_TILE + g.pad) & ((hh + 1) % g.nb == 0)))
        h2 = jnp.where(is_pad, 0.0, h2)
    out_ref[...] = h2


def _mlp(head, stream, first, g, layer, fo, ho, wo, nw, wg, wu, wd, fnw, final):
    d = head.shape[1]
    dff = wg.shape[-1]
    if final:
        grid = (g.batch, g.sb // 2)
        block_of = lambda b, k: b * g.nb + 1 + 2 * k
        out_rows = g.batch * g.sb * SEQ_TILE
        out_spec = pl.BlockSpec((ROW_TILE, d), lambda b, k: (b * (g.sb // 2) + k, 0))
    else:
        grid = (g.batch * g.nb // 2,)
        block_of = lambda i: 2 * i
        out_rows = g.batch * g.nb * SEQ_TILE
        out_spec = pl.BlockSpec((ROW_TILE, d), lambda i: (i, 0))
    const = lambda a: _layer_spec(a, layer, True)
    small = lambda shape: pl.BlockSpec(shape, lambda *_: (0, 0))
    blocks = lambda a: a.reshape(-1, SEQ_TILE, a.shape[-1])
    fo3, ho3 = blocks(fo), blocks(ho)
    return pl.pallas_call(
        functools.partial(_mlp_kernel, first=first, final=final, g=g),
        grid=grid,
        in_specs=[small(head.shape),
                  *_stream_specs(first, g, d, block_of),
                  *_stream_specs(False, g, FOX_WIDTH, block_of),
                  *_stream_specs(False, g, HG_WIDTH, block_of),
                  const(wo), _layer_spec(nw, layer), const(wg), const(wu), const(wd),
                  small((1, d))],
        out_specs=out_spec,
        out_shape=jax.ShapeDtypeStruct((out_rows, d), F32),
        scratch_shapes=[pltpu.VMEM((ROW_TILE, dff), BF16)],
        compiler_params=pltpu.CompilerParams(
            dimension_semantics=("arbitrary",) * len(grid), vmem_limit_bytes=VMEM_LIMIT),
        name="mlp_final" if final else "mlp",
    )(head, stream, stream, fo3, fo3, ho3, ho3, wo, nw, wg, wu, wd, fnw)


def kernel(x, meta, norm_mix_w, w_in, fox_f_bias, hgrn_lb_raw, hgrn_norm_w, w_out,
           norm_ffn_w, w_ffn_gate, w_ffn_up, w_ffn_down, norm_final_w):
    batch, seq, d = x.shape
    depth = w_in.shape[0]
    assert seq % ROW_TILE == 0 and N_META <= SEQ_TILE
    lp = seq + SEQ_TILE
    assert (batch * lp) % ROW_TILE == 0
    padf = SEQ_TILE - N_META
    g = _Geom(batch=batch, nb=lp // SEQ_TILE, sb=seq // SEQ_TILE, pad=padf)

    head = jnp.concatenate([jnp.zeros((padf, d), x.dtype), meta.astype(x.dtype)], axis=0)
    stream = x.reshape(batch * g.sb, SEQ_TILE, d)

    wfox, whg, wff = _split_w_in(w_in.astype(F32))
    wo, wg, wu, wd = (w.astype(F32) for w in (w_out, w_ffn_gate, w_ffn_up, w_ffn_down))
    fb = jnp.pad(fox_f_bias.astype(F32), ((0, 0), (0, LANES - FOX_HEADS)))[:, None]
    lb_raw = hgrn_lb_raw.astype(F32)
    nw_mix, nw_ffn, nw_hg = (a.astype(F32)[:, None]
                             for a in (norm_mix_w, norm_ffn_w, hgrn_norm_w))
    for l in range(depth):
        first, final = l == 0, l == depth - 1
        qt, k, vt, hq, hf, hi, hg, ff, nrm = _inproj(head, stream, first, g, l, nw_mix,
                                                     wfox, whg, wff)
        e, js = _gates(ff, fb, nrm, l, batch, lp, padf)
        fox_o = _fox(js, qt, k, vt, e, batch, lp)
        hgrn_o = _hgrn(hq, hf, hi, hg, lb_raw, nw_hg, l, batch, lp)
        out = _mlp(head, stream, first, g, l, fox_o, hgrn_o, wo, nw_ffn, wg, wu, wd,
                   norm_final_w[None].astype(F32), final)
        stream = out.reshape(-1, SEQ_TILE, d)
    return out.reshape(batch, seq, d)
```
